```python
import jax, jax.numpy as jnp
from jax import lax
import numpy as np

D_MODEL = 1024
BATCH = 4
SEQ = 8192
DEPTH = 2

N_A_LAYERS = max(1, DEPTH // 2)
N_B_LAYERS = DEPTH - N_A_LAYERS
HEAD_DIM = 64
N_HEADS_A = D_MODEL // HEAD_DIM
N_HEADS_B = D_MODEL // HEAD_DIM
DECAY_LORA = 64
AAA_LORA = 64
GATE_LORA = 160
N_SHIFT_MIX = 6
FFN_HIDDEN = -(-8 * D_MODEL // (3 * 256)) * 256
MOBA_BLOCK = 256
MOBA_TOPK = 3
Q_CHUNK = 16
NORM_EPS = 1e-6
GN_EPS = 64e-5
L2_EPS = 1e-12

kernel_name = "yoco_rwkv7_moba_hybrid"


def rms_norm(x, g):
    xf = x.astype(jnp.float32)
    y = xf * lax.rsqrt(jnp.mean(xf * xf, axis=-1, keepdims=True) + NORM_EPS)
    return (y * g.astype(jnp.float32)).astype(x.dtype)


def ada_params(c, w, b, n):
    m = jax.nn.silu(c) @ w + b
    return jnp.split(m[:, None, :], n, axis=-1)


def swiglu(h, w_gate, w_up, w_down):
    return (jax.nn.silu(h @ w_gate) * (h @ w_up)) @ w_down


def _rwkv7_step(S, inp):
    r, w, k, v, a, b = inp
    sa = jnp.einsum('bhij,bhj->bhi', S, a)
    S = S * w[:, :, None, :] + sa[..., None] * b[:, :, None, :] + v[..., None] * k[:, :, None, :]
    y = jnp.einsum('bhij,bhj->bhi', S, r)
    return S, y


def rwkv7_time_mix(h, mu, w_rkv, w0, w1, w2, a0, a1, a2, g1, g2, k_k, k_a, r_k, lnx_g, lnx_b, w_o):
    B, T, D = h.shape
    H, N = N_HEADS_A, HEAD_DIM
    heads = lambda z: z.reshape(B, T, H, N)
    h_prev = jnp.pad(h, ((0, 0), (1, 0), (0, 0)))[:, :-1]
    xs = h[None] + (h_prev - h)[None] * mu[:, None, None, :]
    r, k, v = jnp.einsum('sbtd,sde->sbte', xs[:3], w_rkv)
    xw, xa, xg = xs[3], xs[4], xs[5]
    w = -jax.nn.softplus(-(w0 + jnp.tanh(xw @ w1) @ w2)) - 0.5
    a = jax.nn.sigmoid(a0 + (xa @ a1) @ a2)
    g = jax.nn.sigmoid(xg @ g1) @ g2
    kk = heads(k * k_k).astype(jnp.float32)
    kk = kk / jnp.maximum(jnp.sqrt(jnp.sum(kk * kk, axis=-1, keepdims=True)), L2_EPS)
    k = k * (1 + (a - 1) * k_a)
    decay = jnp.exp(-jnp.exp(w.astype(jnp.float32)))
    tm = lambda z: heads(z).astype(jnp.float32).transpose(1, 0, 2, 3)
    kk_t = kk.transpose(1, 0, 2, 3)
    seqs = (tm(r), tm(decay), tm(k), tm(v), -kk_t, kk_t * tm(a))
    S0 = jnp.zeros((B, H, N, N), jnp.float32)
    _, y = lax.scan(_rwkv7_step, S0, seqs)
    y = y.transpose(1, 0, 2, 3)
    mean = jnp.mean(y, axis=-1, keepdims=True)
    var = jnp.mean(jnp.square(y - mean), axis=-1, keepdims=True)
    yn = ((y - mean) * lax.rsqrt(var + GN_EPS)).reshape(B, T, D)
    yn = (yn * lnx_g.astype(jnp.float32) + lnx_b.astype(jnp.float32)).astype(h.dtype)
    bonus = jnp.sum(heads(r) * heads(k) * r_k, axis=-1, keepdims=True) * heads(v)
    return ((yn + bonus.reshape(B, T, D)) * g) @ w_o


def shared_kv(x, c, kv_norm_g, kv_w_ada, kv_b_ada, kv_w_k, kv_w_v, k_norm_g):
    B, T, D = x.shape
    H, HD = N_HEADS_B, HEAD_DIM
    shift, scale = ada_params(c, kv_w_ada, kv_b_ada, 2)
    h = rms_norm(x, kv_norm_g) * (1 + scale) + shift
    k = rms_norm((h @ kv_w_k).reshape(B, T, H, HD), k_norm_g)
    v = (h @ kv_w_v).reshape(B, T, H, HD)
    nb = -(-T // MOBA_BLOCK)
    pad = nb * MOBA_BLOCK - T
    to_blocks = lambda z: jnp.pad(z, ((0, 0), (0, pad), (0, 0), (0, 0))).reshape(
        B, nb, MOBA_BLOCK, H, HD).transpose(0, 3, 1, 2, 4)
    kb, vb = to_blocks(k), to_blocks(v)
    kmean = jnp.mean(kb.astype(jnp.float32), axis=3).astype(kb.dtype)
    return kb, vb, kmean


def moba_attention(h, w_q, q_norm_g, kb, vb, kmean, w_o):
    B, T, D = h.shape
    H, HD = N_HEADS_B, HEAD_DIM
    nb = kb.shape[2]
    topk = min(MOBA_TOPK, nb)
    nc = T // Q_CHUNK
    sm_scale = HEAD_DIM ** -0.5
    q = rms_norm((h @ w_q).reshape(B, T, H, HD), q_norm_g)
    q_chunks = q.reshape(B, nc, Q_CHUNK, H, HD).transpose(1, 0, 3, 2, 4)
    bi = jnp.arange(B)[:, None, None, None]
    hi = jnp.arange(H)[None, :, None, None]
    slot = jnp.arange(topk)
    blocks = jnp.arange(nb)
    key_off = jnp.arange(MOBA_BLOCK)
    q_off = jnp.arange(Q_CHUNK)

    def chunk(args):
        qc, ci = args
        t0 = ci * Q_CHUNK
        blk = t0 // MOBA_BLOCK
        gate = jnp.einsum('bhqd,bhnd->bhqn', qc, kmean).astype(jnp.float32)
        gate = jnp.where(blocks < blk, gate, -jnp.inf)
        _, idx = lax.top_k(gate, topk)
        k_sel = kb[bi, hi, idx]
        v_sel = vb[bi, hi, idx]
        s_sel = jnp.einsum('bhqd,bhqskd->bhqsk', qc, k_sel).astype(jnp.float32) * sm_scale
        s_sel = jnp.where((slot < blk)[:, None], s_sel, -jnp.inf)
        k_own = lax.dynamic_index_in_dim(kb, blk, axis=2, keepdims=False)
        v_own = lax.dynamic_index_in_dim(vb, blk, axis=2, keepdims=False)
        s_own = jnp.einsum('bhqd,bhkd->bhqk', qc, k_own).astype(jnp.float32) * sm_scale
        causal = (blk * MOBA_BLOCK + key_off)[None, :] <= (t0 + q_off)[:, None]
        s_own = jnp.where(causal, s_own, -jnp.inf)
        s = jnp.concatenate([s_sel.reshape(B, H, Q_CHUNK, topk * MOBA_BLOCK), s_own], axis=-1)
        p = jax.nn.softmax(s, axis=-1).astype(vb.dtype)
        p_sel = p[..., :topk * MOBA_BLOCK].reshape(B, H, Q_CHUNK, topk, MOBA_BLOCK)
        p_own = p[..., topk * MOBA_BLOCK:]
        return (jnp.einsum('bhqsk,bhqskd->bhqd', p_sel, v_sel)
                + jnp.einsum('bhqk,bhkd->bhqd', p_own, v_own))

    o = lax.map(chunk, (q_chunks, jnp.arange(nc)))
    o = o.transpose(1, 0, 3, 2, 4).reshape(B, T, D)
    return o @ w_o


def setup_inputs(seed: int = 0) -> dict:
    key = jax.random.key(seed)
    ks = iter(jax.random.split(key, 40))
    f32 = jnp.float32
    D, F, NA, NBL, HD = D_MODEL, FFN_HIDDEN, N_A_LAYERS, N_B_LAYERS, HEAD_DIM
    nrm = lambda shape, s: jax.random.normal(next(ks), shape, f32) * s
    uni = lambda shape, lo, hi: jax.random.uniform(next(ks), shape, f32, lo, hi)
    return {
        "x": nrm((BATCH, SEQ, D), 1.0),
        "c": nrm((BATCH, D), 1.0),
        "norm_g": 1.0 + nrm((DEPTH, 2, D), 0.1),
        "w_ada": nrm((DEPTH, 2, D, 3 * D), 0.5 * D ** -0.5),
        "b_ada": nrm((DEPTH, 2, 3 * D), 0.02),
        "rw_mu": uni((NA, N_SHIFT_MIX, D), 0.0, 1.0),
        "rw_w_rkv": nrm((NA, 3, D, D), D ** -0.5),
        "rw_w0": uni((NA, D), -6.0, 0.0),
        "rw_w1": nrm((NA, D, DECAY_LORA), D ** -0.5),
        "rw_w2": nrm((NA, DECAY_LORA, D), 0.5 * DECAY_LORA ** -0.5),
        "rw_a0": nrm((NA, D), 0.5),
        "rw_a1": nrm((NA, D, AAA_LORA), D ** -0.5),
        "rw_a2": nrm((NA, AAA_LORA, D), AAA_LORA ** -0.5),
        "rw_g1": nrm((NA, D, GATE_LORA), D ** -0.5),
        "rw_g2": nrm((NA, GATE_LORA, D), GATE_LORA ** -0.5),
        "rw_k_k": 0.85 + nrm((NA, D), 0.1),
        "rw_k_a": 1.0 + nrm((NA, D), 0.1),
        "rw_r_k": nrm((NA, N_HEADS_A, HD), 0.1),
        "rw_lnx_g": 1.0 + nrm((NA, D), 0.1),
        "rw_lnx_b": nrm((NA, D), 0.02),
        "rw_w_o": nrm((NA, D, D), D ** -0.5),
        "ffn_w_gate": nrm((DEPTH, D, F), D ** -0.5),
        "ffn_w_up": nrm((DEPTH, D, F), D ** -0.5),
        "ffn_w_down": nrm((DEPTH, F, D), F ** -0.5),
        "kv_norm_g": 1.0 + nrm((D,), 0.1),
        "kv_w_ada": nrm((D, 2 * D), 0.5 * D ** -0.5),
        "kv_b_ada": nrm((2 * D,), 0.02),
        "kv_w_k": nrm((D, D), D ** -0.5),
        "kv_w_v": nrm((D, D), D ** -0.5),
        "k_norm_g": 1.0 + nrm((HD,), 0.1),
        "mb_w_q": nrm((NBL, D, D), D ** -0.5),
        "mb_q_norm_g": 1.0 + nrm((NBL, HD), 0.1),
        "mb_w_o": nrm((NBL, D, D), D ** -0.5),
    }


def reference(x, c, norm_g, w_ada, b_ada, rw_mu, rw_w_rkv, rw_w0, rw_w1, rw_w2, rw_a0, rw_a1, rw_a2,
              rw_g1, rw_g2, rw_k_k, rw_k_a, rw_r_k, rw_lnx_g, rw_lnx_b, rw_w_o,
              ffn_w_gate, ffn_w_up, ffn_w_down, kv_norm_g, kv_w_ada, kv_b_ada, kv_w_k, kv_w_v,
              k_norm_g, mb_w_q, mb_q_norm_g, mb_w_o):
    kb = vb = kmean = None
    for i in range(DEPTH):
        shift, scale, gate = ada_params(c, w_ada[i, 0], b_ada[i, 0], 3)
        h = rms_norm(x, norm_g[i, 0]) * (1 + scale) + shift
        if i < N_A_LAYERS:
            j = i
            mix = rwkv7_time_mix(h, rw_mu[j], rw_w_rkv[j], rw_w0[j], rw_w1[j], rw_w2[j],
                                 rw_a0[j], rw_a1[j], rw_a2[j], rw_g1[j], rw_g2[j],
                                 rw_k_k[j], rw_k_a[j], rw_r_k[j], rw_lnx_g[j], rw_lnx_b[j], rw_w_o[j])
        else:
            j = i - N_A_LAYERS
            mix = moba_attention(h, mb_w_q[j], mb_q_norm_g[j], kb, vb, kmean, mb_w_o[j])
        x = x + gate * mix
        shift, scale, gate = ada_params(c, w_ada[i, 1], b_ada[i, 1], 3)
        h = rms_norm(x, norm_g[i, 1]) * (1 + scale) + shift
        x = x + gate * swiglu(h, ffn_w_gate[i], ffn_w_up[i], ffn_w_down[i])
        if i == N_A_LAYERS - 1:
            kb, vb, kmean = shared_kv(x, c, kv_norm_g, kv_w_ada, kv_b_ada, kv_w_k, kv_w_v, k_norm_g)
    return x
```

```python
import functools

import jax
import jax.numpy as jnp
from jax import lax
from jax.experimental import pallas as pl
from jax.experimental.pallas import tpu as pltpu

F32 = jnp.float32
BF16 = jnp.bfloat16
HIGHEST = lax.Precision.HIGHEST

D_MODEL = 1024
HEAD_DIM = 64
N_HEADS = D_MODEL // HEAD_DIM
MOBA_BLOCK = 256
MOBA_TOPK = 3
NORM_EPS = 1e-6
GN_EPS = 64e-5
L2_EPS = 1e-12

LANES = 128
HEADS_PER_LANE_GROUP = LANES // HEAD_DIM
VMEM_LIMIT_BYTES = 56 * 1024 * 1024
SCAN_CHUNK = 64
ROW_TILE = 256
SCAN_TIME_BLOCK = 1024


def _cparams(*sem):
    return pltpu.CompilerParams(dimension_semantics=sem, vmem_limit_bytes=VMEM_LIMIT_BYTES)


def _dot(a, b):
    return jnp.dot(a, b, preferred_element_type=F32)


def _dot_nt(a, b, precision=None):
    return lax.dot_general(a, b, (((1,), (1,)), ((), ())), precision=precision,
                           preferred_element_type=F32)


def _dot_tn(a, b):
    return lax.dot_general(a, b, (((0,), (0,)), ((), ())), preferred_element_type=F32)


def _head_ones():
    r = lax.broadcasted_iota(jnp.int32, (LANES, LANES), 0) // HEAD_DIM
    c = lax.broadcasted_iota(jnp.int32, (LANES, LANES), 1) // HEAD_DIM
    return (r == c).astype(F32)


def _head_sum(x):
    ones = _head_ones()
    parts = [jnp.dot(x[:, j * LANES:(j + 1) * LANES], ones, precision=HIGHEST,
                     preferred_element_type=F32) for j in range(x.shape[1] // LANES)]
    return jnp.concatenate(parts, axis=1)


def _ada_ln(x, g, shift, scale):
    inv = lax.rsqrt(jnp.mean(x * x, axis=-1, keepdims=True) + NORM_EPS)
    return (x * inv * g) * (1.0 + scale) + shift


def _ada_kernel(c_ref, w_ref, b_ref, o_ref):
    c = c_ref[...]
    s = c * jax.nn.sigmoid(c)
    o_ref[...] = jnp.dot(s, w_ref[...], precision=HIGHEST, preferred_element_type=F32) + b_ref[...]


def _ada(c, w, b):
    n, d, md = w.shape
    bsz = c.shape[0]
    return pl.pallas_call(
        _ada_kernel,
        out_shape=jax.ShapeDtypeStruct((n, bsz, md), F32),
        grid=(n, md // d),
        in_specs=[pl.BlockSpec((bsz, d), lambda i, j: (0, 0)),
                  pl.BlockSpec((None, d, d), lambda i, j: (i, 0, j)),
                  pl.BlockSpec((None, 1, d), lambda i, j: (i, 0, j))],
        out_specs=pl.BlockSpec((None, bsz, d), lambda i, j: (i, 0, j)),
        compiler_params=_cparams("parallel", "parallel"),
        name="ada",
    )(c, w, b)


def _rwkv_proj_kernel(x_ref, shift_ref, scale_ref, ng_ref, mu_ref, wrkv_ref, w0_ref, w1_ref,
                      w2_ref, a0_ref, a1_ref, a2_ref, g1_ref, g2_ref, kk_ref, ka_ref,
                      r_out, k_out, v_out, ld_out, kkn_out, kb_out, g_out, carry_ref):
    tm = x_ref.shape[0]
    h = _ada_ln(x_ref[...], ng_ref[...], shift_ref[...], scale_ref[...])
    first = pl.program_id(1) == 0
    prev_last = jnp.where(first, 0.0, carry_ref[7:8, :])
    row = lax.broadcasted_iota(jnp.int32, h.shape, 0)
    h_prev = jnp.where(row == 0, prev_last, pltpu.roll(h, 1, axis=0))
    carry_ref[...] = h[tm - 8:tm, :]
    d = h_prev - h
    mix = lambda i: (h + d * mu_ref[i:i + 1, :]).astype(BF16)

    r = _dot(mix(0), wrkv_ref[0])
    k = _dot(mix(1), wrkv_ref[1])
    v = _dot(mix(2), wrkv_ref[2])
    zw = w0_ref[...] + _dot(jnp.tanh(_dot(mix(3), w1_ref[...])).astype(BF16), w2_ref[...])
    a = jax.nn.sigmoid(a0_ref[...] + _dot(_dot(mix(4), a1_ref[...]).astype(BF16), a2_ref[...]))
    g = _dot(jax.nn.sigmoid(_dot(mix(5), g1_ref[...])).astype(BF16), g2_ref[...])

    ld_out[...] = -jnp.exp(F32(-0.5)) * jax.nn.sigmoid(zw)
    kk = k * kk_ref[...]
    nrm = jnp.maximum(jnp.sqrt(_head_sum(kk * kk)), L2_EPS)
    kk = kk / nrm
    r_out[...] = r
    k_out[...] = k * (1.0 + (a - 1.0) * ka_ref[...])
    v_out[...] = v
    kkn_out[...] = kk
    kb_out[...] = kk * a
    g_out[...] = g


def _rwkv_proj(x, shift, scale, ng, mu, wrkv, w0, w1, w2, a0, a1, a2, g1, g2, k_k, k_a):
    bsz, t, d = x.shape
    tm = ROW_TILE
    row = pl.BlockSpec((None, tm, d), lambda b, i: (b, i, 0))
    mod = pl.BlockSpec((None, 1, d), lambda b, i: (b, 0, 0))
    vec = pl.BlockSpec((1, d), lambda b, i: (0, 0))
    full = lambda a: pl.BlockSpec(a.shape, lambda b, i: (0,) * a.ndim)
    out = jax.ShapeDtypeStruct((bsz, t, d), F32)
    return pl.pallas_call(
        _rwkv_proj_kernel,
        out_shape=(out,) * 7,
        grid=(bsz, t // tm),
        in_specs=[row, mod, mod, vec, full(mu), full(wrkv), vec, full(w1), full(w2), vec,
                  full(a1), full(a2), full(g1), full(g2), vec, vec],
        out_specs=(row,) * 7,
        scratch_shapes=[pltpu.VMEM((8, d), F32)],
        compiler_params=_cparams("arbitrary", "arbitrary"),
        name="rwkv_proj",
    )(x, shift, scale, ng, mu, wrkv, w0, w1, w2, a0, a1, a2, g1, g2, k_k, k_a)


def _rwkv_scan_kernel(r_ref, k_ref, v_ref, ld_ref, kk_ref, kb_ref, y_ref, s_ref):
    c = SCAN_CHUNK
    n_chunks = r_ref.shape[0] // c

    @pl.when(pl.program_id(2) == 0)
    def _():
        s_ref[...] = jnp.zeros_like(s_ref)

    ri = lax.broadcasted_iota(jnp.int32, (c, c), 0)
    ci = lax.broadcasted_iota(jnp.int32, (c, c), 1)
    incl = ri >= ci
    strict = ri > ci
    tri = incl.astype(F32)
    eye = (ri == ci).astype(F32)

    def chunk(ic, carry):
        rows = pl.ds(pl.multiple_of(ic * c, c), c)
        ld = ld_ref[rows, :]
        cum = jnp.dot(tri, ld, precision=HIGHEST, preferred_element_type=F32)
        tot = cum[c - 1:c, :]
        g_in = jnp.exp(cum)
        g_prev = jnp.exp(cum - ld)
        g_inv = jnp.exp(-cum)
        g_out = jnp.exp(tot)
        g_rest = jnp.exp(tot - cum)
        kkn = kk_ref[rows, :]
        kb = kb_ref[rows, :]
        kx = k_ref[rows, :]
        vx = v_ref[rows, :]
        at = -kkn * g_prev
        rt = r_ref[rows, :] * g_in
        bh = kb * g_inv
        kh = kx * g_inv
        bc = kb * g_rest
        kc = kx * g_rest
        ys = []
        new_states = []
        for hd in range(HEADS_PER_LANE_GROUP):
            sl = slice(hd * HEAD_DIM, (hd + 1) * HEAD_DIM)
            s0 = carry[hd]
            at_h, rt_h, v_h = at[:, sl], rt[:, sl], vx[:, sl]
            q2 = jnp.concatenate([at_h, rt_h], axis=0)
            ab = _dot_nt(q2, bh[:, sl])
            ak = _dot_nt(q2, kh[:, sl])
            a_ab = jnp.where(strict, ab[:c], 0.0)
            a_rb = jnp.where(incl, ab[c:], 0.0)
            a_ak = jnp.where(strict, ak[:c], 0.0)
            a_rk = jnp.where(incl, ak[c:], 0.0)
            pw = a_ab
            t_inv = eye + pw
            for _ in range(5):
                pw = _dot(pw, pw)
                t_inv = t_inv + _dot(t_inv, pw)
            w1 = _dot(t_inv, at_h)
            u0 = _dot(t_inv, _dot(a_ak, v_h))
            u = _dot_nt(w1, s0) + u0
            ys.append(_dot_nt(rt_h, s0) + _dot(a_rb, u) + _dot(a_rk, v_h))
            new_states.append(s0 * g_out[:, sl] + _dot_tn(u, bc[:, sl]) + _dot_tn(v_h, kc[:, sl]))
        y_ref[rows, :] = jnp.concatenate(ys, axis=1)
        return tuple(new_states)

    init = tuple(s_ref[hd] for hd in range(HEADS_PER_LANE_GROUP))
    final = lax.fori_loop(0, n_chunks, chunk, init)
    for hd in range(HEADS_PER_LANE_GROUP):
        s_ref[hd] = final[hd]


def _rwkv_scan(r, k, v, ld, kkn, kb):
    bsz, t, d = r.shape
    tb = SCAN_TIME_BLOCK
    blk = pl.BlockSpec((None, tb, LANES), lambda b, hp, i: (b, i, hp))
    return pl.pallas_call(
        _rwkv_scan_kernel,
        out_shape=jax.ShapeDtypeStruct((bsz, t, d), F32),
        grid=(bsz, d // LANES, t // tb),
        in_specs=[blk] * 6,
        out_specs=blk,
        scratch_shapes=[pltpu.VMEM((HEADS_PER_LANE_GROUP, HEAD_DIM, HEAD_DIM), F32)],
        compiler_params=_cparams("parallel", "parallel", "arbitrary"),
        name="rwkv_scan",
    )(r, k, v, ld, kkn, kb)


def _rwkv_out_kernel(y_ref, r_ref, k_ref, v_ref, g_ref, x_ref, gate_ref, lg_ref, lb_ref, rk_ref,
                     wo_ref, o_ref):
    y = y_ref[...]
    mean = _head_sum(y) * (1.0 / HEAD_DIM)
    dy = y - mean
    var = _head_sum(dy * dy) * (1.0 / HEAD_DIM)
    yn = dy * lax.rsqrt(var + GN_EPS) * lg_ref[...] + lb_ref[...]
    bonus = _head_sum(r_ref[...] * k_ref[...] * rk_ref[...]) * v_ref[...]
    mix = _dot(((yn + bonus) * g_ref[...]).astype(BF16), wo_ref[...])
    o_ref[...] = x_ref[...] + gate_ref[...] * mix


def _rwkv_out(y, r, k, v, g, x, gate, lnx_g, lnx_b, r_k, wo):
    bsz, t, d = x.shape
    tm = ROW_TILE
    row = pl.BlockSpec((None, tm, d), lambda b, i: (b, i, 0))
    mod = pl.BlockSpec((None, 1, d), lambda b, i: (b, 0, 0))
    vec = pl.BlockSpec((1, d), lambda b, i: (0, 0))
    return pl.pallas_call(
        _rwkv_out_kernel,
        out_shape=jax.ShapeDtypeStruct((bsz, t, d), F32),
        grid=(bsz, t // tm),
        in_specs=[row] * 6 + [mod, vec, vec, vec, pl.BlockSpec((d, d), lambda b, i: (0, 0))],
        out_specs=row,
        compiler_params=_cparams("parallel", "parallel"),
        name="rwkv_out",
    )(y, r, k, v, g, x, gate, lnx_g, lnx_b, r_k, wo)


def _ffn_kernel(x_ref, shift_ref, scale_ref, gate_ref, ng_ref, wg_ref, wu_ref, wd_ref, o_ref):
    x = x_ref[...]
    h = _ada_ln(x, ng_ref[...], shift_ref[...], scale_ref[...]).astype(BF16)
    gt = _dot(h, wg_ref[...])
    up = _dot(h, wu_ref[...])
    act = (gt * jax.nn.sigmoid(gt) * up).astype(BF16)
    o_ref[...] = x + gate_ref[...] * _dot(act, wd_ref[...])


def _ffn(x, shift, scale, gate, ng, wg, wu, wd):
    bsz, t, d = x.shape
    f = wg.shape[1]
    tm = ROW_TILE
    row = pl.BlockSpec((None, tm, d), lambda b, i: (b, i, 0))
    mod = pl.BlockSpec((None, 1, d), lambda b, i: (b, 0, 0))
    vec = pl.BlockSpec((1, d), lambda b, i: (0, 0))
    return pl.pallas_call(
        _ffn_kernel,
        out_shape=jax.ShapeDtypeStruct((bsz, t, d), F32),
        grid=(bsz, t // tm),
        in_specs=[row, mod, mod, mod, vec,
                  pl.BlockSpec((d, f), lambda b, i: (0, 0)),
                  pl.BlockSpec((d, f), lambda b, i: (0, 0)),
                  pl.BlockSpec((f, d), lambda b, i: (0, 0))],
        out_specs=row,
        compiler_params=_cparams("parallel", "parallel"),
        name="ffn",
    )(x, shift, scale, gate, ng, wg, wu, wd)


def _head_rms(z, g):
    ms = _head_sum(z * z) * (1.0 / HEAD_DIM)
    return z * lax.rsqrt(ms + NORM_EPS) * g


def _moba_qkv_kernel(x_ref, qshift_ref, qscale_ref, qng_ref, kshift_ref, kscale_ref, kng_ref,
                     wq_ref, wk_ref, wv_ref, qg_ref, kg_ref, q_out, k_out, v_out, km_out):
    x = x_ref[...]
    xn = x * lax.rsqrt(jnp.mean(x * x, axis=-1, keepdims=True) + NORM_EPS)
    hq = ((xn * qng_ref[...]) * (1.0 + qscale_ref[...]) + qshift_ref[...]).astype(BF16)
    hk = ((xn * kng_ref[...]) * (1.0 + kscale_ref[...]) + kshift_ref[...]).astype(BF16)
    q_out[...] = _head_rms(_dot(hq, wq_ref[...]), qg_ref[...])
    k = _head_rms(_dot(hk, wk_ref[...]), kg_ref[...])
    k_out[...] = k.astype(BF16)
    v_out[...] = _dot(hk, wv_ref[...]).astype(BF16)
    km_out[...] = jnp.mean(k, axis=0, keepdims=True)


def _moba_qkv(x, qshift, qscale, qng, kshift, kscale, kng, wq, wk, wv, qg, kg):
    bsz, t, d = x.shape
    tm = MOBA_BLOCK
    nb = t // tm
    row = pl.BlockSpec((None, tm, d), lambda b, i: (b, i, 0))
    mod = pl.BlockSpec((None, 1, d), lambda b, i: (b, 0, 0))
    vec = pl.BlockSpec((1, d), lambda b, i: (0, 0))
    mat = pl.BlockSpec((d, d), lambda b, i: (0, 0))
    return pl.pallas_call(
        _moba_qkv_kernel,
        out_shape=(jax.ShapeDtypeStruct((bsz, t, d), F32),
                   jax.ShapeDtypeStruct((bsz, t, d), BF16),
                   jax.ShapeDtypeStruct((bsz, t, d), BF16),
                   jax.ShapeDtypeStruct((bsz * nb, 1, d), F32)),
        grid=(bsz, nb),
        in_specs=[row, mod, mod, vec, mod, mod, vec, mat, mat, mat, vec, vec],
        out_specs=(row, row, row, pl.BlockSpec((None, 1, d), lambda b, i: (b * nb + i, 0, 0))),
        compiler_params=_cparams("parallel", "parallel"),
        name="moba_qkv",
    )(x, qshift, qscale, qng, kshift, kscale, kng, wq, wk, wv, qg, kg)


def _moba_attn_kernel(q_ref, k_ref, v_ref, km_ref, o_ref):
    blk = MOBA_BLOCK
    nb = km_ref.shape[0]
    qb = pl.program_id(2)
    q = q_ref[...]
    km = km_ref[...]
    neg_inf = F32(-jnp.inf)
    blk_iota = lax.broadcasted_iota(jnp.int32, (blk, nb), 1)
    valid = blk_iota < qb
    rowi = lax.broadcasted_iota(jnp.int32, (blk, blk), 0)
    coli = lax.broadcasted_iota(jnp.int32, (blk, blk), 1)
    causal = coli <= rowi
    own = pl.ds(pl.multiple_of(qb * blk, blk), blk)
    k_own = k_ref[own, :]
    v_own = v_ref[own, :]

    qs, biases, state = [], [], []
    for hd in range(HEADS_PER_LANE_GROUP):
        sl = slice(hd * HEAD_DIM, (hd + 1) * HEAD_DIM)
        qh = q[:, sl]
        gate = jnp.where(valid, _dot_nt(qh, km[:, sl], precision=HIGHEST), neg_inf)
        sel = jnp.zeros((blk, nb), jnp.bool_)
        for _ in range(MOBA_TOPK):
            mx = jnp.max(gate, axis=1, keepdims=True)
            idx = jnp.min(jnp.where(gate == mx, blk_iota, nb), axis=1, keepdims=True)
            hit = blk_iota == idx
            sel = jnp.logical_or(sel, hit)
            gate = jnp.where(hit, neg_inf, gate)
        biases.append(jnp.where(jnp.logical_and(sel, valid), 0.0, neg_inf))
        qh = (qh * (HEAD_DIM ** -0.5)).astype(BF16)
        qs.append(qh)
        s = jnp.where(causal, _dot_nt(qh, k_own[:, sl]), neg_inf)
        m = jnp.max(s, axis=1, keepdims=True)
        p = jnp.exp(s - m)
        state += [m, jnp.sum(p, axis=1, keepdims=True), _dot(p.astype(BF16), v_own[:, sl])]

    def past(n, carry):
        rows = pl.ds(pl.multiple_of(n * blk, blk), blk)
        kn = k_ref[rows, :]
        vn = v_ref[rows, :]
        out = []
        for hd in range(HEADS_PER_LANE_GROUP):
            sl = slice(hd * HEAD_DIM, (hd + 1) * HEAD_DIM)
            m, l, acc = carry[3 * hd:3 * hd + 3]
            bcol = jnp.sum(jnp.where(blk_iota == n, biases[hd], 0.0), axis=1, keepdims=True)
            s = _dot_nt(qs[hd], kn[:, sl]) + bcol
            m_new = jnp.maximum(m, jnp.max(s, axis=1, keepdims=True))
            alpha = jnp.exp(m - m_new)
            p = jnp.exp(s - m_new)
            out += [m_new, alpha * l + jnp.sum(p, axis=1, keepdims=True),
                    alpha * acc + _dot(p.astype(BF16), vn[:, sl])]
        return tuple(out)

    final = lax.fori_loop(0, qb, past, tuple(state))
    o_ref[...] = jnp.concatenate(
        [final[3 * hd + 2] / final[3 * hd + 1] for hd in range(HEADS_PER_LANE_GROUP)], axis=1)


def _moba_attn(q, k, v, km):
    bsz, t, d = q.shape
    nb = t // MOBA_BLOCK
    qblk = pl.BlockSpec((None, MOBA_BLOCK, LANES), lambda b, hp, i: (b, i, hp))
    seq = pl.BlockSpec((None, t, LANES), lambda b, hp, i: (b, 0, hp))
    return pl.pallas_call(
        _moba_attn_kernel,
        out_shape=jax.ShapeDtypeStruct((bsz, t, d), F32),
        grid=(bsz, d // LANES, nb),
        in_specs=[qblk, seq, seq, pl.BlockSpec((None, nb, LANES), lambda b, hp, i: (b, 0, hp))],
        out_specs=qblk,
        compiler_params=_cparams("parallel", "parallel", "arbitrary"),
        name="moba_attn",
    )(q, k, v, km)


def _proj_res_kernel(o_ref, x_ref, gate_ref, w_ref, out_ref):
    out_ref[...] = x_ref[...] + gate_ref[...] * _dot(o_ref[...].astype(BF16), w_ref[...])


def _proj_res(o, x, gate, w):
    bsz, t, d = x.shape
    tm = ROW_TILE
    row = pl.BlockSpec((None, tm, d), lambda b, i: (b, i, 0))
    return pl.pallas_call(
        _proj_res_kernel,
        out_shape=jax.ShapeDtypeStruct((bsz, t, d), F32),
        grid=(bsz, t // tm),
        in_specs=[row, row, pl.BlockSpec((None, 1, d), lambda b, i: (b, 0, 0)),
                  pl.BlockSpec((d, d), lambda b, i: (0, 0))],
        out_specs=row,
        compiler_params=_cparams("parallel", "parallel"),
        name="proj_res",
    )(o, x, gate, w)


def _pad_rank(w_down, w_up):
    rank = w_down.shape[1]
    pad = -rank % LANES
    return (jnp.pad(w_down, ((0, 0), (0, pad))).astype(BF16),
            jnp.pad(w_up, ((0, pad), (0, 0))).astype(BF16))


def kernel(x, c, norm_g, w_ada, b_ada, rw_mu, rw_w_rkv, rw_w0, rw_w1, rw_w2, rw_a0, rw_a1, rw_a2, rw_g1, rw_g2, rw_k_k, rw_k_a, rw_r_k, rw_lnx_g, rw_lnx_b, rw_w_o, ffn_w_gate, ffn_w_up, ffn_w_down, kv_norm_g, kv_w_ada, kv_b_ada, kv_w_k, kv_w_v, k_norm_g, mb_w_q, mb_q_norm_g, mb_w_o):
    depth = norm_g.shape[0]
    d = x.shape[-1]
    assert depth == 2 and rw_mu.shape[0] == 1 and mb_w_q.shape[0] == 1 and d == D_MODEL

    mods = _ada(c, w_ada.reshape(2 * depth, d, 3 * d), b_ada.reshape(2 * depth, 1, 3 * d))
    kv_mods = _ada(c, kv_w_ada[None], kv_b_ada[None, None])[0]
    part = lambda m, j: m[:, None, j * d:(j + 1) * d]
    vec = lambda a: a.reshape(1, d)

    m = mods[0]
    w1, w2 = _pad_rank(rw_w1[0], rw_w2[0])
    a1, a2 = _pad_rank(rw_a1[0], rw_a2[0])
    g1, g2 = _pad_rank(rw_g1[0], rw_g2[0])
    r, k, v, ld, kkn, kb, g = _rwkv_proj(
        x, part(m, 0), part(m, 1), vec(norm_g[0, 0]), rw_mu[0], rw_w_rkv[0].astype(BF16),
        vec(rw_w0[0]), w1, w2, vec(rw_a0[0]), a1, a2, g1, g2, vec(rw_k_k[0]), vec(rw_k_a[0]))
    y = _rwkv_scan(r, k, v, ld, kkn, kb)
    x = _rwkv_out(y, r, k, v, g, x, part(m, 2), vec(rw_lnx_g[0]), vec(rw_lnx_b[0]),
                  vec(rw_r_k[0]), rw_w_o[0].astype(BF16))
    m = mods[1]
    x = _ffn(x, part(m, 0), part(m, 1), part(m, 2), vec(norm_g[0, 1]),
             ffn_w_gate[0].astype(BF16), ffn_w_up[0].astype(BF16), ffn_w_down[0].astype(BF16))

    m = mods[2]
    q, kx, vx, km = _moba_qkv(
        x, part(m, 0), part(m, 1), vec(norm_g[1, 0]), part(kv_mods, 0), part(kv_mods, 1),
        vec(kv_norm_g), mb_w_q[0].astype(BF16), kv_w_k.astype(BF16), kv_w_v.astype(BF16),
        vec(jnp.tile(mb_q_norm_g[0], N_HEADS)), vec(jnp.tile(k_norm_g, N_HEADS)))
    o = _moba_attn(q, kx, vx, km.reshape(x.shape[0], -1, d))
    x = _proj_res(o, x, part(m, 2), mb_w_o[0].astype(BF16))
    m = mods[3]
    x = _ffn(x, part(m, 0), part(m, 1), part(m, 2), vec(norm_g[1, 1]),
             ffn_w_gate[1].astype(BF16), ffn_w_up[1].astype(BF16), ffn_w_down[1].astype(BF16))
    return x
```

```python
import functools

import jax
import jax.numpy as jnp
from jax import lax
from jax.experimental import pallas as pl
from jax.experimental.pallas import tpu as pltpu

F32 = jnp.float32
BF16 = jnp.bfloat16
HIGHEST = lax.Precision.HIGHEST

D_MODEL = 1024
HEAD_DIM = 64
N_HEADS = D_MODEL // HEAD_DIM
MOBA_BLOCK = 256
MOBA_TOPK = 3
NORM_EPS = 1e-6
GN_EPS = 64e-5
L2_EPS = 1e-12

LANES = 128
HEADS_PER_LANE_GROUP = LANES // HEAD_DIM
VMEM_LIMIT_BYTES = 56 * 1024 * 1024
SCAN_CHUNK = 64
ROW_TILE = 256
SCAN_TIME_BLOCK = 1024
SCAN_GROUPS_PER_STEP = 2
SCAN_UNROLL = 4


def _cparams(*sem):
    return pltpu.CompilerParams(dimension_semantics=sem, vmem_limit_bytes=VMEM_LIMIT_BYTES)


def _dot(a, b):
    return jnp.dot(a, b, preferred_element_type=F32)


def _dot_nt(a, b, precision=None):
    return lax.dot_general(a, b, (((1,), (1,)), ((), ())), precision=precision,
                           preferred_element_type=F32)


def _dot_tn(a, b):
    return lax.dot_general(a, b, (((0,), (0,)), ((), ())), preferred_element_type=F32)


def _head_ones():
    r = lax.broadcasted_iota(jnp.int32, (LANES, LANES), 0) // HEAD_DIM
    c = lax.broadcasted_iota(jnp.int32, (LANES, LANES), 1) // HEAD_DIM
    return (r == c).astype(F32)


def _head_sum(x):
    ones = _head_ones()
    parts = [jnp.dot(x[:, j * LANES:(j + 1) * LANES], ones, precision=HIGHEST,
                     preferred_element_type=F32) for j in range(x.shape[1] // LANES)]
    return jnp.concatenate(parts, axis=1)


def _ada_ln(x, g, shift, scale):
    inv = lax.rsqrt(jnp.mean(x * x, axis=-1, keepdims=True) + NORM_EPS)
    return (x * inv * g) * (1.0 + scale) + shift


def _ada_kernel(c_ref, w_ref, b_ref, o_ref):
    c = c_ref[...]
    s = c * jax.nn.sigmoid(c)
    o_ref[...] = jnp.dot(s, w_ref[...], precision=HIGHEST, preferred_element_type=F32) + b_ref[...]


def _ada(c, w, b):
    n, d, md = w.shape
    bsz = c.shape[0]
    return pl.pallas_call(
        _ada_kernel,
        out_shape=jax.ShapeDtypeStruct((n, bsz, md), F32),
        grid=(n, md // d),
        in_specs=[pl.BlockSpec((bsz, d), lambda i, j: (0, 0)),
                  pl.BlockSpec((None, d, d), lambda i, j: (i, 0, j)),
                  pl.BlockSpec((None, 1, d), lambda i, j: (i, 0, j))],
        out_specs=pl.BlockSpec((None, bsz, d), lambda i, j: (i, 0, j)),
        compiler_params=_cparams("parallel", "parallel"),
        name="ada",
    )(c, w, b)


def _rwkv_proj_kernel(x_ref, shift_ref, scale_ref, ng_ref, mu_ref, wrkv_ref, w0_ref, w1_ref,
                      w2_ref, a0_ref, a1_ref, a2_ref, g1_ref, g2_ref, kk_ref, ka_ref,
                      r_out, k_out, v_out, ld_out, kkn_out, kb_out, g_out, carry_ref):
    tm = x_ref.shape[0]
    h = _ada_ln(x_ref[...], ng_ref[...], shift_ref[...], scale_ref[...])
    first = pl.program_id(1) == 0
    prev_last = jnp.where(first, 0.0, carry_ref[7:8, :])
    row = lax.broadcasted_iota(jnp.int32, h.shape, 0)
    h_prev = jnp.where(row == 0, prev_last, pltpu.roll(h, 1, axis=0))
    carry_ref[...] = h[tm - 8:tm, :]
    d = h_prev - h
    mix = lambda i: (h + d * mu_ref[i:i + 1, :]).astype(BF16)

    r = _dot(mix(0), wrkv_ref[0])
    k = _dot(mix(1), wrkv_ref[1])
    v = _dot(mix(2), wrkv_ref[2])
    zw = w0_ref[...] + _dot(jnp.tanh(_dot(mix(3), w1_ref[...])).astype(BF16), w2_ref[...])
    a = jax.nn.sigmoid(a0_ref[...] + _dot(_dot(mix(4), a1_ref[...]).astype(BF16), a2_ref[...]))
    g = _dot(jax.nn.sigmoid(_dot(mix(5), g1_ref[...])).astype(BF16), g2_ref[...])

    ld_out[...] = -jnp.exp(F32(-0.5)) * jax.nn.sigmoid(zw)
    kk = k * kk_ref[...]
    nrm = jnp.maximum(jnp.sqrt(_head_sum(kk * kk)), L2_EPS)
    kk = kk / nrm
    r_out[...] = r
    k_out[...] = k * (1.0 + (a - 1.0) * ka_ref[...])
    v_out[...] = v
    kkn_out[...] = kk
    kb_out[...] = kk * a
    g_out[...] = g


def _rwkv_proj(x, shift, scale, ng, mu, wrkv, w0, w1, w2, a0, a1, a2, g1, g2, k_k, k_a):
    bsz, t, d = x.shape
    tm = ROW_TILE
    row = pl.BlockSpec((None, tm, d), lambda b, i: (b, i, 0))
    mod = pl.BlockSpec((None, 1, d), lambda b, i: (b, 0, 0))
    vec = pl.BlockSpec((1, d), lambda b, i: (0, 0))
    full = lambda a: pl.BlockSpec(a.shape, lambda b, i: (0,) * a.ndim)
    out = jax.ShapeDtypeStruct((bsz, t, d), F32)
    return pl.pallas_call(
        _rwkv_proj_kernel,
        out_shape=(out,) * 7,
        grid=(bsz, t // tm),
        in_specs=[row, mod, mod, vec, full(mu), full(wrkv), vec, full(w1), full(w2), vec,
                  full(a1), full(a2), full(g1), full(g2), vec, vec],
        out_specs=(row,) * 7,
        scratch_shapes=[pltpu.VMEM((8, d), F32)],
        compiler_params=_cparams("arbitrary", "arbitrary"),
        name="rwkv_proj",
    )(x, shift, scale, ng, mu, wrkv, w0, w1, w2, a0, a1, a2, g1, g2, k_k, k_a)


def _rwkv_scan_kernel(r_ref, k_ref, v_ref, ld_ref, kk_ref, kb_ref, y_ref,
                      s_ref, ry_ref, y1_ref, rr_ref, q_ref, g_ref):
    c = SCAN_CHUNK
    n_chunks = r_ref.shape[0] // c
    n_groups = r_ref.shape[1] // LANES
    two_c = HEADS_PER_LANE_GROUP * c

    @pl.when(pl.program_id(2) == 0)
    def _():
        s_ref[...] = jnp.zeros_like(s_ref)

    ri = lax.broadcasted_iota(jnp.int32, (two_c, two_c), 0)
    ci = lax.broadcasted_iota(jnp.int32, (two_c, two_c), 1)
    same_head = (ri // c) == (ci // c)
    strict = jnp.logical_and(same_head, ri > ci)
    incl = jnp.logical_and(same_head, ri >= ci)
    eye = (ri == ci).astype(F32)
    ti = lax.broadcasted_iota(jnp.int32, (c, c), 0)
    tj = lax.broadcasted_iota(jnp.int32, (c, c), 1)
    tri = (ti >= tj).astype(F32)
    head0 = lax.broadcasted_iota(jnp.int32, (c, LANES), 1) < HEAD_DIM

    def stack(z):
        return jnp.concatenate([jnp.where(head0, z, 0.0), jnp.where(head0, 0.0, z)], axis=0)

    def prepare(chains):
        each = lambda f, *cols: [f(*args) for args in zip(*cols)]
        rows = [pl.ds(pl.multiple_of(ic * c, c), c) for ic, _ in chains]
        lanes = [slice(gi * LANES, (gi + 1) * LANES) for _, gi in chains]
        load = lambda ref: each(lambda rw, ln: ref[rw, ln], rows, lanes)
        ld = load(ld_ref)
        cum = each(lambda z: jnp.dot(tri, z, precision=HIGHEST, preferred_element_type=F32), ld)
        tot = each(lambda z: z[c - 1:c, :], cum)
        kb, kx = load(kb_ref), load(k_ref)
        g_inv = each(lambda z: jnp.exp(-z), cum)
        g_rest = each(lambda t_, z: jnp.exp(t_ - z), tot, cum)
        at = each(lambda kn, z, l_: stack(-kn * jnp.exp(z - l_)), load(kk_ref), cum, ld)
        rt = each(lambda r_, z: stack(r_ * jnp.exp(z)), load(r_ref), cum)
        bh = each(lambda x_, g_: stack(x_ * g_), kb, g_inv)
        kh = each(lambda x_, g_: stack(x_ * g_), kx, g_inv)
        bc = each(lambda x_, g_: stack(x_ * g_), kb, g_rest)
        kc = each(lambda x_, g_: stack(x_ * g_), kx, g_rest)
        vs = each(stack, load(v_ref))
        qa = each(lambda a_, r_: jnp.concatenate([a_, r_], axis=0), at, rt)
        mb = each(_dot_nt, qa, bh)
        mk = each(_dot_nt, qa, kh)
        a_ab = each(lambda z: jnp.where(strict, z[:two_c], 0.0), mb)
        a_rb = each(lambda z: jnp.where(incl, z[two_c:], 0.0), mb)
        a_ak = each(lambda z: jnp.where(strict, z[:two_c], 0.0), mk)
        a_rk = each(lambda z: jnp.where(incl, z[two_c:], 0.0), mk)
        pw = a_ab
        t_inv = each(lambda z: eye + z, pw)
        for _ in range(5):
            pw = each(_dot, pw, pw)
            t_inv = each(lambda t_, p_: t_ + _dot(t_, p_), t_inv, pw)
        w1 = each(_dot, t_inv, at)
        av = each(_dot, a_ak, vs)
        u0 = each(_dot, t_inv, av)
        ry = each(lambda r_, a_, w_: r_ + _dot(a_, w_), rt, a_rb, w1)
        y1 = each(lambda a_, u_, b_, v_: _dot(a_, u_) + _dot(b_, v_), a_rb, u0, a_rk, vs)
        rr = each(_dot_tn, w1, bc)
        qq = each(lambda u_, b_, v_, k_: _dot_tn(u_, b_) + _dot_tn(v_, k_), u0, bc, vs, kc)
        for j, (ic, gi) in enumerate(chains):
            ry_ref[gi, ic] = ry[j]
            y1_ref[gi, ic] = y1[j][:c] + y1[j][c:]
            rr_ref[gi, ic] = rr[j]
            q_ref[gi, ic] = qq[j]
            g_ref[gi, ic] = jnp.broadcast_to(jnp.exp(tot[j]), (8, LANES))

    def prepare_group(ig, carry):
        prepare([(ig * SCAN_UNROLL + u, gi) for u in range(SCAN_UNROLL) for gi in range(n_groups)])
        return carry

    lax.fori_loop(0, n_chunks // SCAN_UNROLL, prepare_group, 0)

    def advance(ic, states):
        rows = pl.ds(pl.multiple_of(ic * c, c), c)
        groups = range(n_groups)
        sr = [_dot(states[gi], rr_ref[gi, ic]) for gi in groups]
        ys = [_dot_nt(ry_ref[gi, ic], states[gi]) for gi in groups]
        for gi in groups:
            y_ref[rows, gi * LANES:(gi + 1) * LANES] = ys[gi][:c] + ys[gi][c:] + y1_ref[gi, ic]
        return tuple(states[gi] * g_ref[gi, ic, 0:1, :] + sr[gi] + q_ref[gi, ic] for gi in groups)

    final = lax.fori_loop(0, n_chunks, advance, tuple(s_ref[gi] for gi in range(n_groups)))
    for gi in range(n_groups):
        s_ref[gi] = final[gi]


def _rwkv_scan(r, k, v, ld, kkn, kb):
    bsz, t, d = r.shape
    tb = SCAN_TIME_BLOCK
    n_groups = SCAN_GROUPS_PER_STEP
    n_chunks = tb // SCAN_CHUNK
    two_c = HEADS_PER_LANE_GROUP * SCAN_CHUNK
    blk = pl.BlockSpec((None, tb, n_groups * LANES), lambda b, hp, i: (b, i, hp))
    mats = pltpu.VMEM((n_groups, n_chunks, two_c, LANES), F32)
    return pl.pallas_call(
        _rwkv_scan_kernel,
        out_shape=jax.ShapeDtypeStruct((bsz, t, d), F32),
        grid=(bsz, d // (n_groups * LANES), t // tb),
        in_specs=[blk] * 6,
        out_specs=blk,
        scratch_shapes=[pltpu.VMEM((n_groups, two_c, LANES), F32), mats,
                        pltpu.VMEM((n_groups, n_chunks, SCAN_CHUNK, LANES), F32), mats, mats,
                        pltpu.VMEM((n_groups, n_chunks, 8, LANES), F32)],
        compiler_params=_cparams("parallel", "parallel", "arbitrary"),
        name="rwkv_scan",
    )(r, k, v, ld, kkn, kb)


def _rwkv_out_kernel(y_ref, r_ref, k_ref, v_ref, g_ref, x_ref, gate_ref, lg_ref, lb_ref, rk_ref,
                     wo_ref, o_ref):
    y = y_ref[...]
    mean = _head_sum(y) * (1.0 / HEAD_DIM)
    dy = y - mean
    var = _head_sum(dy * dy) * (1.0 / HEAD_DIM)
    yn = dy * lax.rsqrt(var + GN_EPS) * lg_ref[...] + lb_ref[...]
    bonus = _head_sum(r_ref[...] * k_ref[...] * rk_ref[...]) * v_ref[...]
    mix = _dot(((yn + bonus) * g_ref[...]).astype(BF16), wo_ref[...])
    o_ref[...] = x_ref[...] + gate_ref[...] * mix


def _rwkv_out(y, r, k, v, g, x, gate, lnx_g, lnx_b, r_k, wo):
    bsz, t, d = x.shape
    tm = ROW_TILE
    row = pl.BlockSpec((None, tm, d), lambda b, i: (b, i, 0))
    mod = pl.BlockSpec((None, 1, d), lambda b, i: (b, 0, 0))
    vec = pl.BlockSpec((1, d), lambda b, i: (0, 0))
    return pl.pallas_call(
        _rwkv_out_kernel,
        out_shape=jax.ShapeDtypeStruct((bsz, t, d), F32),
        grid=(bsz, t // tm),
        in_specs=[row] * 6 + [mod, vec, vec, vec, pl.BlockSpec((d, d), lambda b, i: (0, 0))],
        out_specs=row,
        compiler_params=_cparams("parallel", "parallel"),
        name="rwkv_out",
    )(y, r, k, v, g, x, gate, lnx_g, lnx_b, r_k, wo)


def _ffn_kernel(x_ref, shift_ref, scale_ref, gate_ref, ng_ref, wg_ref, wu_ref, wd_ref, o_ref):
    x = x_ref[...]
    h = _ada_ln(x, ng_ref[...], shift_ref[...], scale_ref[...]).astype(BF16)
    gt = _dot(h, wg_ref[...])
    up = _dot(h, wu_ref[...])
    act = (gt * jax.nn.sigmoid(gt) * up).astype(BF16)
    o_ref[...] = x + gate_ref[...] * _dot(act, wd_ref[...])


def _ffn(x, shift, scale, gate, ng, wg, wu, wd):
    bsz, t, d = x.shape
    f = wg.shape[1]
    tm = ROW_TILE
    row = pl.BlockSpec((None, tm, d), lambda b, i: (b, i, 0))
    mod = pl.BlockSpec((None, 1, d), lambda b, i: (b, 0, 0))
    vec = pl.BlockSpec((1, d), lambda b, i: (0, 0))
    return pl.pallas_call(
        _ffn_kernel,
        out_shape=jax.ShapeDtypeStruct((bsz, t, d), F32),
        grid=(bsz, t // tm),
        in_specs=[row, mod, mod, mod, vec,
                  pl.BlockSpec((d, f), lambda b, i: (0, 0)),
                  pl.BlockSpec((d, f), lambda b, i: (0, 0)),
                  pl.BlockSpec((f, d), lambda b, i: (0, 0))],
        out_specs=row,
        compiler_params=_cparams("parallel", "parallel"),
        name="ffn",
    )(x, shift, scale, gate, ng, wg, wu, wd)


def _head_rms(z, g):
    ms = _head_sum(z * z) * (1.0 / HEAD_DIM)
    return z * lax.rsqrt(ms + NORM_EPS) * g


def _moba_qkv_kernel(x_ref, qshift_ref, qscale_ref, qng_ref, kshift_ref, kscale_ref, kng_ref,
                     wq_ref, wk_ref, wv_ref, qg_ref, kg_ref, qt_out, k_out, vt_out, km_out):
    x = x_ref[...]
    xn = x * lax.rsqrt(jnp.mean(x * x, axis=-1, keepdims=True) + NORM_EPS)
    hq = ((xn * qng_ref[...]) * (1.0 + qscale_ref[...]) + qshift_ref[...]).astype(BF16)
    hk = ((xn * kng_ref[...]) * (1.0 + kscale_ref[...]) + kshift_ref[...]).astype(BF16)
    qt_out[...] = _head_rms(_dot(hq, wq_ref[...]), qg_ref[...]).T
    k = _head_rms(_dot(hk, wk_ref[...]), kg_ref[...])
    k_out[...] = k.astype(BF16)
    vt_out[...] = _dot(hk, wv_ref[...]).T.astype(BF16)
    km_out[...] = jnp.mean(k, axis=0, keepdims=True)


def _moba_qkv(x, qshift, qscale, qng, kshift, kscale, kng, wq, wk, wv, qg, kg):
    bsz, t, d = x.shape
    tm = MOBA_BLOCK
    nb = t // tm
    row = pl.BlockSpec((None, tm, d), lambda b, i: (b, i, 0))
    mod = pl.BlockSpec((None, 1, d), lambda b, i: (b, 0, 0))
    vec = pl.BlockSpec((1, d), lambda b, i: (0, 0))
    mat = pl.BlockSpec((d, d), lambda b, i: (0, 0))
    return pl.pallas_call(
        _moba_qkv_kernel,
        out_shape=(jax.ShapeDtypeStruct((bsz, d, t), F32),
                   jax.ShapeDtypeStruct((bsz, t, d), BF16),
                   jax.ShapeDtypeStruct((bsz, nb, d, tm), BF16),
                   jax.ShapeDtypeStruct((bsz * nb, 1, d), F32)),
        grid=(bsz, nb),
        in_specs=[row, mod, mod, vec, mod, mod, vec, mat, mat, mat, vec, vec],
        out_specs=(pl.BlockSpec((None, d, tm), lambda b, i: (b, 0, i)), row,
                   pl.BlockSpec((None, None, d, tm), lambda b, i: (b, i, 0, 0)),
                   pl.BlockSpec((None, 1, d), lambda b, i: (b * nb + i, 0, 0))),
        compiler_params=_cparams("parallel", "parallel"),
        name="moba_qkv",
    )(x, qshift, qscale, qng, kshift, kscale, kng, wq, wk, wv, qg, kg)


def _moba_attn_kernel(qt_ref, k_ref, vt_ref, km_ref, o_ref, qs_ref, bias_ref, sa_ref, sb_ref,
                      p_ref):
    blk = MOBA_BLOCK
    nb = km_ref.shape[0]
    qb = pl.program_id(2)
    qt = qt_ref[...]
    km = km_ref[...]
    neg_inf = F32(-jnp.inf)
    chan_head = lax.broadcasted_iota(jnp.int32, qt.shape, 0) // HEAD_DIM
    blk_iota = lax.broadcasted_iota(jnp.int32, (nb, blk), 0)
    valid = blk_iota < qb
    key_i = lax.broadcasted_iota(jnp.int32, (blk, blk), 0)
    qry_i = lax.broadcasted_iota(jnp.int32, (blk, blk), 1)
    causal = key_i <= qry_i
    k_own = k_ref[pl.ds(pl.multiple_of(qb * blk, blk), blk), :]
    vt_own = vt_ref[qb]

    state = []
    for hd in range(HEADS_PER_LANE_GROUP):
        qt_h = jnp.where(chan_head == hd, qt, 0.0)
        gate = jnp.where(valid, jnp.dot(km, qt_h, precision=HIGHEST, preferred_element_type=F32),
                         neg_inf)
        sel = jnp.zeros((nb, blk), jnp.bool_)
        for _ in range(MOBA_TOPK):
            mx = jnp.max(gate, axis=0, keepdims=True)
            idx = jnp.min(jnp.where(gate == mx, blk_iota, nb), axis=0, keepdims=True)
            hit = blk_iota == idx
            sel = jnp.logical_or(sel, hit)
            gate = jnp.where(hit, neg_inf, gate)
        bias_ref[hd] = jnp.where(jnp.logical_and(sel, valid), 0.0, neg_inf)
        qs = (qt_h * (HEAD_DIM ** -0.5)).astype(BF16)
        qs_ref[hd] = qs
        s = jnp.where(causal, _dot(k_own, qs), neg_inf)
        m = jnp.max(s, axis=0, keepdims=True)
        p = jnp.exp(s - m)
        state += [m, jnp.sum(p, axis=0, keepdims=True),
                  _dot(vt_own[hd * HEAD_DIM:(hd + 1) * HEAD_DIM, :], p.astype(BF16))]

    heads = range(HEADS_PER_LANE_GROUP)

    def put_scores(n, dst_ref):
        kn = k_ref[pl.ds(pl.multiple_of(n * blk, blk), blk), :]
        for hd in heads:
            dst_ref[hd] = _dot(kn, qs_ref[hd]) + bias_ref[hd, pl.ds(n, 1), :]

    def weighted_values(n, p):
        vtn = vt_ref[n]
        return [_dot(vtn[hd * HEAD_DIM:(hd + 1) * HEAD_DIM, :], p[hd]) for hd in heads]

    def softmax_step(src_ref, m, l):
        s = [src_ref[hd] for hd in heads]
        m_new = [jnp.maximum(m[hd], jnp.max(s[hd], axis=0, keepdims=True)) for hd in heads]
        alpha = [jnp.exp(m[hd] - m_new[hd]) for hd in heads]
        p = [jnp.exp(s[hd] - m_new[hd]) for hd in heads]
        l_new = [alpha[hd] * l[hd] + jnp.sum(p[hd], axis=0, keepdims=True) for hd in heads]
        return m_new, l_new, alpha, [p[hd].astype(BF16) for hd in heads]

    def past_pair(j, carry):
        m, l, acc, alpha_prev = carry
        a = 2 * j
        put_scores(a + 1, sb_ref)
        pv_prev = weighted_values(jnp.maximum(a - 1, 0), [p_ref[hd] for hd in heads])
        m, l, alpha_a, p_a = softmax_step(sa_ref, m, l)
        put_scores(jnp.minimum(a + 2, nb - 1), sa_ref)
        pv_a = weighted_values(a, p_a)
        m, l, alpha_b, p_b = softmax_step(sb_ref, m, l)
        for hd in heads:
            p_ref[hd] = p_b[hd]
        acc = [alpha_a[hd] * (alpha_prev[hd] * acc[hd] + pv_prev[hd]) + pv_a[hd] for hd in heads]
        return m, l, acc, alpha_b

    init = ([state[3 * hd] for hd in heads], [state[3 * hd + 1] for hd in heads],
            [state[3 * hd + 2] for hd in heads], [jnp.ones((1, blk), F32) for _ in heads])
    n_pairs = (qb + 1) // 2
    p_ref[...] = jnp.zeros_like(p_ref)
    put_scores(0, sa_ref)
    m, l, acc, alpha_prev = lax.fori_loop(0, n_pairs, past_pair, init)
    pv_last = weighted_values(jnp.maximum(2 * n_pairs - 1, 0), [p_ref[hd] for hd in heads])
    ot = jnp.concatenate([(alpha_prev[hd] * acc[hd] + pv_last[hd]) / l[hd] for hd in heads], axis=0)
    o_ref[...] = ot.T


def _moba_attn(qt, k, vt, km):
    bsz, d, t = qt.shape
    nb = t // MOBA_BLOCK
    return pl.pallas_call(
        _moba_attn_kernel,
        out_shape=jax.ShapeDtypeStruct((bsz, t, d), F32),
        grid=(bsz, d // LANES, nb),
        in_specs=[pl.BlockSpec((None, LANES, MOBA_BLOCK), lambda b, hp, i: (b, hp, i)),
                  pl.BlockSpec((None, t, LANES), lambda b, hp, i: (b, 0, hp)),
                  pl.BlockSpec((None, nb, LANES, MOBA_BLOCK), lambda b, hp, i: (b, 0, hp, 0)),
                  pl.BlockSpec((None, nb, LANES), lambda b, hp, i: (b, 0, hp))],
        out_specs=pl.BlockSpec((None, MOBA_BLOCK, LANES), lambda b, hp, i: (b, i, hp)),
        scratch_shapes=[pltpu.VMEM((HEADS_PER_LANE_GROUP, LANES, MOBA_BLOCK), BF16),
                        pltpu.VMEM((HEADS_PER_LANE_GROUP, nb, MOBA_BLOCK), F32),
                        pltpu.VMEM((HEADS_PER_LANE_GROUP, MOBA_BLOCK, MOBA_BLOCK), F32),
                        pltpu.VMEM((HEADS_PER_LANE_GROUP, MOBA_BLOCK, MOBA_BLOCK), F32),
                        pltpu.VMEM((HEADS_PER_LANE_GROUP, MOBA_BLOCK, MOBA_BLOCK), BF16)],
        compiler_params=_cparams("parallel", "parallel", "arbitrary"),
        name="moba_attn",
    )(qt, k, vt, km)


def _proj_res_kernel(o_ref, x_ref, gate_ref, w_ref, out_ref):
    out_ref[...] = x_ref[...] + gate_ref[...] * _dot(o_ref[...].astype(BF16), w_ref[...])


def _proj_res(o, x, gate, w):
    bsz, t, d = x.shape
    tm = ROW_TILE
    row = pl.BlockSpec((None, tm, d), lambda b, i: (b, i, 0))
    return pl.pallas_call(
        _proj_res_kernel,
        out_shape=jax.ShapeDtypeStruct((bsz, t, d), F32),
        grid=(bsz, t // tm),
        in_specs=[row, row, pl.BlockSpec((None, 1, d), lambda b, i: (b, 0, 0)),
                  pl.BlockSpec((d, d), lambda b, i: (0, 0))],
        out_specs=row,
        compiler_params=_cparams("parallel", "parallel"),
        name="proj_res",
    )(o, x, gate, w)


def _pad_rank(w_down, w_up):
    rank = w_down.shape[1]
    pad = -rank % LANES
    return (jnp.pad(w_down, ((0, 0), (0, pad))).astype(BF16),
            jnp.pad(w_up, ((0, pad), (0, 0))).astype(BF16))


def kernel(x, c, norm_g, w_ada, b_ada, rw_mu, rw_w_rkv, rw_w0, rw_w1, rw_w2, rw_a0, rw_a1, rw_a2, rw_g1, rw_g2, rw_k_k, rw_k_a, rw_r_k, rw_lnx_g, rw_lnx_b, rw_w_o, ffn_w_gate, ffn_w_up, ffn_w_down, kv_norm_g, kv_w_ada, kv_b_ada, kv_w_k, kv_w_v, k_norm_g, mb_w_q, mb_q_norm_g, mb_w_o):
    depth = norm_g.shape[0]
    d = x.shape[-1]
    assert depth == 2 and rw_mu.shape[0] == 1 and mb_w_q.shape[0] == 1 and d == D_MODEL

    mods = _ada(c, w_ada.reshape(2 * depth, d, 3 * d), b_ada.reshape(2 * depth, 1, 3 * d))
    kv_mods = _ada(c, kv_w_ada[None], kv_b_ada[None, None])[0]
    part = lambda m, j: m[:, None, j * d:(j + 1) * d]
    vec = lambda a: a.reshape(1, d)

    m = mods[0]
    w1, w2 = _pad_rank(rw_w1[0], rw_w2[0])
    a1, a2 = _pad_rank(rw_a1[0], rw_a2[0])
    g1, g2 = _pad_rank(rw_g1[0], rw_g2[0])
    r, k, v, ld, kkn, kb, g = _rwkv_proj(
        x, part(m, 0), part(m, 1), vec(norm_g[0, 0]), rw_mu[0], rw_w_rkv[0].astype(BF16),
        vec(rw_w0[0]), w1, w2, vec(rw_a0[0]), a1, a2, g1, g2, vec(rw_k_k[0]), vec(rw_k_a[0]))
    y = _rwkv_scan(r, k, v, ld, kkn, kb)
    x = _rwkv_out(y, r, k, v, g, x, part(m, 2), vec(rw_lnx_g[0]), vec(rw_lnx_b[0]),
                  vec(rw_r_k[0]), rw_w_o[0].astype(BF16))
    m = mods[1]
    x = _ffn(x, part(m, 0), part(m, 1), part(m, 2), vec(norm_g[0, 1]),
             ffn_w_gate[0].astype(BF16), ffn_w_up[0].astype(BF16), ffn_w_down[0].astype(BF16))

    m = mods[2]
    q, kx, vx, km = _moba_qkv(
        x, part(m, 0), part(m, 1), vec(norm_g[1, 0]), part(kv_mods, 0), part(kv_mods, 1),
        vec(kv_norm_g), mb_w_q[0].astype(BF16), kv_w_k.astype(BF16), kv_w_v.astype(BF16),
        vec(jnp.tile(mb_q_norm_g[0], N_HEADS)), vec(jnp.tile(k_norm_g, N_HEADS)))
    o = _moba_attn(q, kx, vx, km.reshape(x.shape[0], -1, d))
    x = _proj_res(o, x, part(m, 2), mb_w_o[0].astype(BF16))
    m = mods[3]
    x = _ffn(x, part(m, 0), part(m, 1), part(m, 2), vec(norm_g[1, 1]),
             ffn_w_gate[1].astype(BF16), ffn_w_up[1].astype(BF16), ffn_w_down[1].astype(BF16))
    return x
```

```python
import functools

import jax
import jax.numpy as jnp
from jax import lax
from jax.experimental import pallas as pl
from jax.experimental.pallas import tpu as pltpu

F32 = jnp.float32
BF16 = jnp.bfloat16
HIGHEST = lax.Precision.HIGHEST

D_MODEL = 1024
HEAD_DIM = 64
N_HEADS = D_MODEL // HEAD_DIM
MOBA_BLOCK = 256
MOBA_TOPK = 3
NORM_EPS = 1e-6
GN_EPS = 64e-5
L2_EPS = 1e-12
LOG2_E = 1.4426950408889634

LANES = 128
HEADS_PER_LANE_GROUP = LANES // HEAD_DIM
VMEM_LIMIT_BYTES = 56 * 1024 * 1024
SCAN_CHUNK = 64
ROW_TILE = 256
SCAN_TIME_BLOCK = 512
SCAN_GROUPS_PER_STEP = 4
ATTN_GROUPS_PER_STEP = 1
SUM_ROWS = 16
SCAN_UNROLL = 2


def _cparams(*sem):
    return pltpu.CompilerParams(dimension_semantics=sem, vmem_limit_bytes=VMEM_LIMIT_BYTES)


def _dot(a, b):
    return jnp.dot(a, b, preferred_element_type=F32)


def _dot_nt(a, b, precision=None):
    return lax.dot_general(a, b, (((1,), (1,)), ((), ())), precision=precision,
                           preferred_element_type=F32)


def _dot_tn(a, b):
    return lax.dot_general(a, b, (((0,), (0,)), ((), ())), preferred_element_type=F32)


def _head_ones():
    r = lax.broadcasted_iota(jnp.int32, (LANES, LANES), 0) // HEAD_DIM
    c = lax.broadcasted_iota(jnp.int32, (LANES, LANES), 1) // HEAD_DIM
    return (r == c).astype(F32)


def _head_sum(x):
    ones = _head_ones()
    parts = [jnp.dot(x[:, j * LANES:(j + 1) * LANES], ones, precision=HIGHEST,
                     preferred_element_type=F32) for j in range(x.shape[1] // LANES)]
    return jnp.concatenate(parts, axis=1)


def _ada_ln(x, g, shift, scale):
    inv = lax.rsqrt(jnp.mean(x * x, axis=-1, keepdims=True) + NORM_EPS)
    return (x * inv * g) * (1.0 + scale) + shift


def _ada_kernel(c_ref, w_ref, b_ref, o_ref):
    c = c_ref[...]
    s = c * jax.nn.sigmoid(c)
    o_ref[...] = jnp.dot(s, w_ref[...], precision=HIGHEST, preferred_element_type=F32) + b_ref[...]


def _ada(c, w, b):
    n, d, md = w.shape
    bsz = c.shape[0]
    return pl.pallas_call(
        _ada_kernel,
        out_shape=jax.ShapeDtypeStruct((n, bsz, md), F32),
        grid=(n, md // d),
        in_specs=[pl.BlockSpec((bsz, d), lambda i, j: (0, 0)),
                  pl.BlockSpec((None, d, d), lambda i, j: (i, 0, j)),
                  pl.BlockSpec((None, 1, d), lambda i, j: (i, 0, j))],
        out_specs=pl.BlockSpec((None, bsz, d), lambda i, j: (i, 0, j)),
        compiler_params=_cparams("parallel", "parallel"),
        name="ada",
    )(c, w, b)


def _rwkv_proj_kernel(x_ref, shift_ref, scale_ref, ng_ref, mu_ref, wrkv_ref, w0_ref, w1_ref,
                      w2_ref, a0_ref, a1_ref, a2_ref, g1_ref, g2_ref, kk_ref, ka_ref,
                      r_out, k_out, v_out, ld_out, kkn_out, kb_out, g_out, carry_ref):
    tm = x_ref.shape[0]
    h = _ada_ln(x_ref[...], ng_ref[...], shift_ref[...], scale_ref[...])
    first = pl.program_id(1) == 0
    prev_last = jnp.where(first, 0.0, carry_ref[7:8, :])
    row = lax.broadcasted_iota(jnp.int32, h.shape, 0)
    h_prev = jnp.where(row == 0, prev_last, pltpu.roll(h, 1, axis=0))
    carry_ref[...] = h[tm - 8:tm, :]
    d = h_prev - h
    mix = lambda i: (h + d * mu_ref[i:i + 1, :]).astype(BF16)

    r = _dot(mix(0), wrkv_ref[0])
    k = _dot(mix(1), wrkv_ref[1])
    v = _dot(mix(2), wrkv_ref[2])
    zw = w0_ref[...] + _dot(jnp.tanh(_dot(mix(3), w1_ref[...])).astype(BF16), w2_ref[...])
    a = jax.nn.sigmoid(a0_ref[...] + _dot(_dot(mix(4), a1_ref[...]).astype(BF16), a2_ref[...]))
    g = _dot(jax.nn.sigmoid(_dot(mix(5), g1_ref[...])).astype(BF16), g2_ref[...])

    ld_out[...] = -jnp.exp(F32(-0.5)) * jax.nn.sigmoid(zw)
    kk = k * kk_ref[...]
    nrm = jnp.maximum(jnp.sqrt(_head_sum(kk * kk)), L2_EPS)
    kk = kk / nrm
    r_out[...] = r
    k_out[...] = k * (1.0 + (a - 1.0) * ka_ref[...])
    v_out[...] = v
    kkn_out[...] = kk
    kb_out[...] = kk * a
    g_out[...] = g


def _rwkv_proj(x, shift, scale, ng, mu, wrkv, w0, w1, w2, a0, a1, a2, g1, g2, k_k, k_a):
    bsz, t, d = x.shape
    tm = ROW_TILE
    row = pl.BlockSpec((None, tm, d), lambda b, i: (b, i, 0))
    mod = pl.BlockSpec((None, 1, d), lambda b, i: (b, 0, 0))
    vec = pl.BlockSpec((1, d), lambda b, i: (0, 0))
    full = lambda a: pl.BlockSpec(a.shape, lambda b, i: (0,) * a.ndim)
    out = jax.ShapeDtypeStruct((bsz, t, d), F32)
    return pl.pallas_call(
        _rwkv_proj_kernel,
        out_shape=(out,) * 7,
        grid=(bsz, t // tm),
        in_specs=[row, mod, mod, vec, full(mu), full(wrkv), vec, full(w1), full(w2), vec,
                  full(a1), full(a2), full(g1), full(g2), vec, vec],
        out_specs=(row,) * 7,
        scratch_shapes=[pltpu.VMEM((8, d), F32)],
        compiler_params=_cparams("arbitrary", "arbitrary"),
        name="rwkv_proj",
    )(x, shift, scale, ng, mu, wrkv, w0, w1, w2, a0, a1, a2, g1, g2, k_k, k_a)


def _rwkv_scan_kernel(r_ref, k_ref, v_ref, ld_ref, kk_ref, kb_ref, y_ref,
                      s_ref, ry_ref, y1_ref, rr_ref, q_ref, g_ref):
    c = SCAN_CHUNK
    n_chunks = r_ref.shape[0] // c
    n_groups = r_ref.shape[1] // LANES
    two_c = HEADS_PER_LANE_GROUP * c

    @pl.when(pl.program_id(2) == 0)
    def _():
        s_ref[...] = jnp.zeros_like(s_ref)

    ri = lax.broadcasted_iota(jnp.int32, (two_c, two_c), 0)
    ci = lax.broadcasted_iota(jnp.int32, (two_c, two_c), 1)
    same_head = (ri // c) == (ci // c)
    strict = jnp.logical_and(same_head, ri > ci)
    incl = jnp.logical_and(same_head, ri >= ci)
    eye = (ri == ci).astype(F32)
    ti = lax.broadcasted_iota(jnp.int32, (c, c), 0)
    tj = lax.broadcasted_iota(jnp.int32, (c, c), 1)
    tri = (ti >= tj).astype(F32)
    head0 = lax.broadcasted_iota(jnp.int32, (c, LANES), 1) < HEAD_DIM

    def stack(z):
        return jnp.concatenate([jnp.where(head0, z, 0.0), jnp.where(head0, 0.0, z)], axis=0)

    def prepare(chains):
        each = lambda f, *cols: [f(*args) for args in zip(*cols)]
        rows = [pl.ds(pl.multiple_of(ic * c, c), c) for ic, _ in chains]
        lanes = [slice(gi * LANES, (gi + 1) * LANES) for _, gi in chains]
        load = lambda ref: each(lambda rw, ln: ref[rw, ln], rows, lanes)
        ld = load(ld_ref)
        cum = each(lambda z: jnp.dot(tri, z, precision=HIGHEST, preferred_element_type=F32), ld)
        tot = each(lambda z: z[c - 1:c, :], cum)
        kb, kx = load(kb_ref), load(k_ref)
        g_inv = each(lambda z: jnp.exp(-z), cum)
        g_rest = each(lambda t_, z: jnp.exp(t_ - z), tot, cum)
        at = each(lambda kn, z, l_: stack(-kn * jnp.exp(z - l_)), load(kk_ref), cum, ld)
        rt = each(lambda r_, z: stack(r_ * jnp.exp(z)), load(r_ref), cum)
        bh = each(lambda x_, g_: stack(x_ * g_), kb, g_inv)
        kh = each(lambda x_, g_: stack(x_ * g_), kx, g_inv)
        bc = each(lambda x_, g_: stack(x_ * g_), kb, g_rest)
        kc = each(lambda x_, g_: stack(x_ * g_), kx, g_rest)
        vs = each(stack, load(v_ref))
        bf = lambda zs: [z.astype(BF16) for z in zs]
        at_b, bc_b, vs_b = bf(at), bf(bc), bf(vs)
        qa = each(lambda a_, r_: jnp.concatenate([a_, r_.astype(BF16)], axis=0), at_b, rt)
        mb = each(_dot_nt, qa, bf(bh))
        mk = each(_dot_nt, qa, bf(kh))
        a_rb = bf(each(lambda z: jnp.where(incl, z[two_c:], 0.0), mb))
        a_ak = bf(each(lambda z: jnp.where(strict, z[:two_c], 0.0), mk))
        a_rk = bf(each(lambda z: jnp.where(incl, z[two_c:], 0.0), mk))
        pw = each(lambda z: jnp.where(strict, z[:two_c], 0.0), mb)
        t_inv = each(lambda z: eye + z, pw)
        for _ in range(5):
            pw_b = bf(pw)
            pw = each(_dot, pw_b, pw_b)
            t_inv = each(lambda t_, p_: t_ + _dot(t_.astype(BF16), p_.astype(BF16)), t_inv, pw)
        t_inv = bf(t_inv)
        w1 = bf(each(_dot, t_inv, at_b))
        av = bf(each(_dot, a_ak, vs_b))
        u0 = bf(each(_dot, t_inv, av))
        ry = each(lambda r_, a_, w_: r_ + _dot(a_, w_), rt, a_rb, w1)
        y1 = each(lambda a_, u_, b_, v_: _dot(a_, u_) + _dot(b_, v_), a_rb, u0, a_rk, vs_b)
        rr = each(_dot_tn, w1, bc_b)
        qq = each(lambda u_, b_, v_, k_: _dot_tn(u_, b_) + _dot_tn(v_, k_), u0, bc_b, vs_b, bf(kc))
        for j, (ic, gi) in enumerate(chains):
            ry_ref[gi, ic] = ry[j].astype(BF16)
            y1_ref[gi, ic] = y1[j][:c] + y1[j][c:]
            rr_ref[gi, ic] = rr[j].astype(BF16)
            q_ref[gi, ic] = qq[j]
            g_ref[gi, ic] = jnp.broadcast_to(jnp.exp(tot[j]), (8, LANES))

    def prepare_group(ig, carry):
        prepare([(ig * SCAN_UNROLL + u, gi) for u in range(SCAN_UNROLL) for gi in range(n_groups)])
        return carry

    lax.fori_loop(0, n_chunks // SCAN_UNROLL, prepare_group, 0)

    def advance(ic, states):
        rows = pl.ds(pl.multiple_of(ic * c, c), c)
        groups = range(n_groups)
        s_b = [states[gi].astype(BF16) for gi in groups]
        sr = [_dot(s_b[gi], rr_ref[gi, ic]) for gi in groups]
        ys = [_dot_nt(ry_ref[gi, ic], s_b[gi]) for gi in groups]
        for gi in groups:
            y_ref[rows, gi * LANES:(gi + 1) * LANES] = ys[gi][:c] + ys[gi][c:] + y1_ref[gi, ic]
        return tuple(states[gi] * g_ref[gi, ic, 0:1, :] + sr[gi] + q_ref[gi, ic] for gi in groups)

    final = lax.fori_loop(0, n_chunks, advance, tuple(s_ref[gi] for gi in range(n_groups)))
    for gi in range(n_groups):
        s_ref[gi] = final[gi]


def _rwkv_scan(r, k, v, ld, kkn, kb):
    bsz, t, d = r.shape
    tb = SCAN_TIME_BLOCK
    n_groups = SCAN_GROUPS_PER_STEP
    n_chunks = tb // SCAN_CHUNK
    two_c = HEADS_PER_LANE_GROUP * SCAN_CHUNK
    blk = pl.BlockSpec((None, tb, n_groups * LANES), lambda b, hp, i: (b, i, hp))
    mats = lambda dt: pltpu.VMEM((n_groups, n_chunks, two_c, LANES), dt)
    return pl.pallas_call(
        _rwkv_scan_kernel,
        out_shape=jax.ShapeDtypeStruct((bsz, t, d), F32),
        grid=(bsz, d // (n_groups * LANES), t // tb),
        in_specs=[blk] * 6,
        out_specs=blk,
        scratch_shapes=[pltpu.VMEM((n_groups, two_c, LANES), F32), mats(BF16),
                        pltpu.VMEM((n_groups, n_chunks, SCAN_CHUNK, LANES), F32), mats(BF16),
                        mats(F32),
                        pltpu.VMEM((n_groups, n_chunks, 8, LANES), F32)],
        compiler_params=_cparams("parallel", "parallel", "arbitrary"),
        name="rwkv_scan",
    )(r, k, v, ld, kkn, kb)


def _rwkv_out_kernel(y_ref, r_ref, k_ref, v_ref, g_ref, x_ref, gate_ref, lg_ref, lb_ref, rk_ref,
                     wo_ref, o_ref):
    y = y_ref[...]
    mean = _head_sum(y) * (1.0 / HEAD_DIM)
    dy = y - mean
    var = _head_sum(dy * dy) * (1.0 / HEAD_DIM)
    yn = dy * lax.rsqrt(var + GN_EPS) * lg_ref[...] + lb_ref[...]
    bonus = _head_sum(r_ref[...] * k_ref[...] * rk_ref[...]) * v_ref[...]
    mix = _dot(((yn + bonus) * g_ref[...]).astype(BF16), wo_ref[...])
    o_ref[...] = x_ref[...] + gate_ref[...] * mix


def _rwkv_out(y, r, k, v, g, x, gate, lnx_g, lnx_b, r_k, wo):
    bsz, t, d = x.shape
    tm = ROW_TILE
    row = pl.BlockSpec((None, tm, d), lambda b, i: (b, i, 0))
    mod = pl.BlockSpec((None, 1, d), lambda b, i: (b, 0, 0))
    vec = pl.BlockSpec((1, d), lambda b, i: (0, 0))
    return pl.pallas_call(
        _rwkv_out_kernel,
        out_shape=jax.ShapeDtypeStruct((bsz, t, d), F32),
        grid=(bsz, t // tm),
        in_specs=[row] * 6 + [mod, vec, vec, vec, pl.BlockSpec((d, d), lambda b, i: (0, 0))],
        out_specs=row,
        compiler_params=_cparams("parallel", "parallel"),
        name="rwkv_out",
    )(y, r, k, v, g, x, gate, lnx_g, lnx_b, r_k, wo)


def _ffn_kernel(x_ref, shift_ref, scale_ref, gate_ref, ng_ref, wg_ref, wu_ref, wd_ref, o_ref):
    x = x_ref[...]
    h = _ada_ln(x, ng_ref[...], shift_ref[...], scale_ref[...]).astype(BF16)
    gt = _dot(h, wg_ref[...])
    up = _dot(h, wu_ref[...])
    act = (gt * jax.nn.sigmoid(gt) * up).astype(BF16)
    o_ref[...] = x + gate_ref[...] * _dot(act, wd_ref[...])


def _ffn(x, shift, scale, gate, ng, wg, wu, wd):
    bsz, t, d = x.shape
    f = wg.shape[1]
    tm = ROW_TILE
    row = pl.BlockSpec((None, tm, d), lambda b, i: (b, i, 0))
    mod = pl.BlockSpec((None, 1, d), lambda b, i: (b, 0, 0))
    vec = pl.BlockSpec((1, d), lambda b, i: (0, 0))
    return pl.pallas_call(
        _ffn_kernel,
        out_shape=jax.ShapeDtypeStruct((bsz, t, d), F32),
        grid=(bsz, t // tm),
        in_specs=[row, mod, mod, mod, vec,
                  pl.BlockSpec((d, f), lambda b, i: (0, 0)),
                  pl.BlockSpec((d, f), lambda b, i: (0, 0)),
                  pl.BlockSpec((f, d), lambda b, i: (0, 0))],
        out_specs=row,
        compiler_params=_cparams("parallel", "parallel"),
        name="ffn",
    )(x, shift, scale, gate, ng, wg, wu, wd)


def _head_rms(z, g):
    ms = _head_sum(z * z) * (1.0 / HEAD_DIM)
    return z * lax.rsqrt(ms + NORM_EPS) * g


def _moba_qkv_kernel(x_ref, qshift_ref, qscale_ref, qng_ref, kshift_ref, kscale_ref, kng_ref,
                     wq_ref, wk_ref, wv_ref, qg_ref, kg_ref, qt_out, k_out, vt_out, km_out):
    x = x_ref[...]
    xn = x * lax.rsqrt(jnp.mean(x * x, axis=-1, keepdims=True) + NORM_EPS)
    hq = ((xn * qng_ref[...]) * (1.0 + qscale_ref[...]) + qshift_ref[...]).astype(BF16)
    hk = ((xn * kng_ref[...]) * (1.0 + kscale_ref[...]) + kshift_ref[...]).astype(BF16)
    qt_out[...] = _head_rms(_dot(hq, wq_ref[...]), qg_ref[...]).T
    k = _head_rms(_dot(hk, wk_ref[...]), kg_ref[...])
    k_out[...] = k.astype(BF16)
    vt_out[...] = _dot(hk, wv_ref[...]).T.astype(BF16)
    km_out[...] = jnp.mean(k, axis=0, keepdims=True)


def _moba_qkv(x, qshift, qscale, qng, kshift, kscale, kng, wq, wk, wv, qg, kg):
    bsz, t, d = x.shape
    tm = MOBA_BLOCK
    nb = t // tm
    row = pl.BlockSpec((None, tm, d), lambda b, i: (b, i, 0))
    mod = pl.BlockSpec((None, 1, d), lambda b, i: (b, 0, 0))
    vec = pl.BlockSpec((1, d), lambda b, i: (0, 0))
    mat = pl.BlockSpec((d, d), lambda b, i: (0, 0))
    return pl.pallas_call(
        _moba_qkv_kernel,
        out_shape=(jax.ShapeDtypeStruct((bsz, d, t), F32),
                   jax.ShapeDtypeStruct((bsz, t, d), BF16),
                   jax.ShapeDtypeStruct((bsz, nb, d, tm), BF16),
                   jax.ShapeDtypeStruct((bsz * nb, 1, d), F32)),
        grid=(bsz, nb),
        in_specs=[row, mod, mod, vec, mod, mod, vec, mat, mat, mat, vec, vec],
        out_specs=(pl.BlockSpec((None, d, tm), lambda b, i: (b, 0, i)), row,
                   pl.BlockSpec((None, None, d, tm), lambda b, i: (b, i, 0, 0)),
                   pl.BlockSpec((None, 1, d), lambda b, i: (b * nb + i, 0, 0))),
        compiler_params=_cparams("parallel", "parallel"),
        name="moba_qkv",
    )(x, qshift, qscale, qng, kshift, kscale, kng, wq, wk, wv, qg, kg)


def _moba_attn_kernel(qt_ref, k_ref, vt_ref, bias_ref, o_ref, qs_ref, sa_ref, sb_ref, p_ref):
    blk = MOBA_BLOCK
    nb = bias_ref.shape[1]
    qb = pl.program_id(2)
    heads = range(qs_ref.shape[0])
    group = lambda hd: slice(hd // HEADS_PER_LANE_GROUP * LANES, (hd // HEADS_PER_LANE_GROUP + 1) * LANES)
    neg_inf = F32(-jnp.inf)
    chan_head = lax.broadcasted_iota(jnp.int32, (LANES, blk), 0) // HEAD_DIM
    key_i = lax.broadcasted_iota(jnp.int32, (blk, blk), 0)
    qry_i = lax.broadcasted_iota(jnp.int32, (blk, blk), 1)
    causal = key_i <= qry_i
    q_scale = F32(HEAD_DIM ** -0.5 * LOG2_E)
    for hd in heads:
        keep = chan_head == hd % HEADS_PER_LANE_GROUP
        qs_ref[hd] = (jnp.where(keep, qt_ref[group(hd), :], 0.0) * q_scale).astype(BF16)

    def put_scores(n, dst_ref):
        rows = pl.ds(pl.multiple_of(n * blk, blk), blk)
        for hd in heads:
            dst_ref[hd] = _dot(k_ref[rows, group(hd)], qs_ref[hd])

    def bias_rows(n):
        return [bias_ref[hd, pl.ds(n, 1), :] for hd in heads]

    ones_rows = jnp.ones((SUM_ROWS, blk), BF16)

    def weighted_values(n, p, rows):
        vtn = vt_ref[n]
        lhs = [jnp.concatenate([vtn[hd * HEAD_DIM:(hd + 1) * HEAD_DIM, :], ones_rows], axis=0)
               for hd in heads]
        return [jnp.where(rows[hd] == 0.0, _dot(lhs[hd], p[hd]), 0.0) for hd in heads]

    def softmax_step(src_ref, m, rows):
        s = [src_ref[hd] for hd in heads]
        m_new = [jnp.maximum(m[hd], jnp.max(s[hd], axis=0, keepdims=True) + rows[hd]) for hd in heads]
        alpha = [jnp.exp2(m[hd] - m_new[hd]) for hd in heads]
        p = [jnp.exp2(s[hd] - m_new[hd]).astype(BF16) for hd in heads]
        return m_new, alpha, p

    past_of = lambda t: jnp.clip(t - 1, 0, nb - 1)
    values_of = lambda t: jnp.where(t == 0, qb, past_of(t))
    rows_of = lambda t: [jnp.where(t == 0, 0.0, r) for r in bias_rows(past_of(t))]

    def tile_pair(j, carry):
        m, acc, alpha_prev = carry
        a, b = 2 * j, 2 * j + 1
        prev = jnp.maximum(a - 1, 0)
        rows_a, rows_b = rows_of(a), rows_of(b)
        put_scores(past_of(b), sb_ref)
        pv_prev = weighted_values(values_of(prev), [p_ref[hd] for hd in heads], rows_of(prev))
        m, alpha_a, p_a = softmax_step(sa_ref, m, rows_a)
        put_scores(past_of(b + 1), sa_ref)
        pv_a = weighted_values(values_of(a), p_a, rows_a)
        m, alpha_b, p_b = softmax_step(sb_ref, m, rows_b)
        for hd in heads:
            p_ref[hd] = p_b[hd]
        acc = [alpha_a[hd] * (alpha_prev[hd] * acc[hd] + pv_prev[hd]) + pv_a[hd] for hd in heads]
        return m, acc, alpha_b

    own = pl.ds(pl.multiple_of(qb * blk, blk), blk)
    for hd in heads:
        sa_ref[hd] = jnp.where(causal, _dot(k_ref[own, group(hd)], qs_ref[hd]), neg_inf)
    p_ref[...] = jnp.zeros_like(p_ref)
    init = ([jnp.full((1, blk), neg_inf, F32) for _ in heads],
            [jnp.zeros((HEAD_DIM + SUM_ROWS, blk), F32) for _ in heads],
            [jnp.ones((1, blk), F32) for _ in heads])
    n_pairs = (qb + 2) // 2
    m, acc, alpha_prev = lax.fori_loop(0, n_pairs, tile_pair, init)
    last = 2 * n_pairs - 1
    pv_last = weighted_values(values_of(last), [p_ref[hd] for hd in heads], rows_of(last))
    acc = [alpha_prev[hd] * acc[hd] + pv_last[hd] for hd in heads]
    ot = jnp.concatenate([acc[hd][:HEAD_DIM] / acc[hd][HEAD_DIM:HEAD_DIM + 1] for hd in heads], axis=0)
    o_ref[...] = ot.T


def _moba_attn(qt, k, vt, bias):
    bsz, d, t = qt.shape
    nb = t // MOBA_BLOCK
    width = ATTN_GROUPS_PER_STEP * LANES
    hpg = ATTN_GROUPS_PER_STEP * HEADS_PER_LANE_GROUP
    return pl.pallas_call(
        _moba_attn_kernel,
        out_shape=jax.ShapeDtypeStruct((bsz, t, d), F32),
        grid=(bsz, d // width, nb),
        in_specs=[pl.BlockSpec((None, width, MOBA_BLOCK), lambda b, hp, i: (b, hp, i)),
                  pl.BlockSpec((None, t, width), lambda b, hp, i: (b, 0, hp)),
                  pl.BlockSpec((None, nb, width, MOBA_BLOCK), lambda b, hp, i: (b, 0, hp, 0)),
                  pl.BlockSpec((None, None, hpg, nb, MOBA_BLOCK), lambda b, hp, i: (b, i, hp, 0, 0))],
        out_specs=pl.BlockSpec((None, MOBA_BLOCK, width), lambda b, hp, i: (b, i, hp)),
        scratch_shapes=[pltpu.VMEM((hpg, LANES, MOBA_BLOCK), BF16),
                        pltpu.VMEM((hpg, MOBA_BLOCK, MOBA_BLOCK), F32),
                        pltpu.VMEM((hpg, MOBA_BLOCK, MOBA_BLOCK), F32),
                        pltpu.VMEM((hpg, MOBA_BLOCK, MOBA_BLOCK), BF16)],
        compiler_params=_cparams("parallel", "parallel", "arbitrary"),
        name="moba_attn",
    )(qt, k, vt, bias)


def _moba_select_kernel(qt_ref, km_ref, bias_ref):
    nb = km_ref.shape[0]
    blk = qt_ref.shape[1]
    qb = pl.program_id(1)
    neg_inf = F32(-jnp.inf)
    rows = HEADS_PER_LANE_GROUP * nb
    row_head = lax.broadcasted_iota(jnp.int32, (rows, LANES), 0) // nb
    chan_head = lax.broadcasted_iota(jnp.int32, (rows, LANES), 1) // HEAD_DIM
    gates = []
    for g in range(qt_ref.shape[0] // LANES):
        km = km_ref[:, g * LANES:(g + 1) * LANES]
        km2 = jnp.where(row_head == chan_head, jnp.concatenate([km] * HEADS_PER_LANE_GROUP, axis=0), 0.0)
        gates.append(jnp.dot(km2, qt_ref[g * LANES:(g + 1) * LANES, :], precision=HIGHEST,
                             preferred_element_type=F32))
    gate = jnp.concatenate(gates, axis=0).reshape(N_HEADS, nb, blk)
    blk_iota = lax.broadcasted_iota(jnp.int32, gate.shape, 1)
    valid = blk_iota < qb
    gate = jnp.where(valid, gate, neg_inf)
    sel = jnp.zeros(gate.shape, jnp.bool_)
    for _ in range(MOBA_TOPK):
        mx = jnp.max(gate, axis=1, keepdims=True)
        idx = jnp.min(jnp.where(gate == mx, blk_iota, nb), axis=1, keepdims=True)
        hit = blk_iota == idx
        sel = jnp.logical_or(sel, hit)
        gate = jnp.where(hit, neg_inf, gate)
    bias_ref[...] = jnp.where(jnp.logical_and(sel, valid), 0.0, neg_inf)


def _moba_select(qt, km):
    bsz, d, t = qt.shape
    nb = km.shape[1]
    return pl.pallas_call(
        _moba_select_kernel,
        out_shape=jax.ShapeDtypeStruct((bsz, nb, N_HEADS, nb, MOBA_BLOCK), F32),
        grid=(bsz, nb),
        in_specs=[pl.BlockSpec((None, d, MOBA_BLOCK), lambda b, i: (b, 0, i)),
                  pl.BlockSpec((None, nb, d), lambda b, i: (b, 0, 0))],
        out_specs=pl.BlockSpec((None, None, N_HEADS, nb, MOBA_BLOCK), lambda b, i: (b, i, 0, 0, 0)),
        compiler_params=_cparams("parallel", "parallel"),
        name="moba_select",
    )(qt, km)


def _proj_res_kernel(o_ref, x_ref, gate_ref, w_ref, out_ref):
    out_ref[...] = x_ref[...] + gate_ref[...] * _dot(o_ref[...].astype(BF16), w_ref[...])


def _proj_res(o, x, gate, w):
    bsz, t, d = x.shape
    tm = ROW_TILE
    row = pl.BlockSpec((None, tm, d), lambda b, i: (b, i, 0))
    return pl.pallas_call(
        _proj_res_kernel,
        out_shape=jax.ShapeDtypeStruct((bsz, t, d), F32),
        grid=(bsz, t // tm),
        in_specs=[row, row, pl.BlockSpec((None, 1, d), lambda b, i: (b, 0, 0)),
                  pl.BlockSpec((d, d), lambda b, i: (0, 0))],
        out_specs=row,
        compiler_params=_cparams("parallel", "parallel"),
        name="proj_res",
    )(o, x, gate, w)


def _pad_rank(w_down, w_up):
    rank = w_down.shape[1]
    pad = -rank % LANES
    return (jnp.pad(w_down, ((0, 0), (0, pad))).astype(BF16),
            jnp.pad(w_up, ((0, pad), (0, 0))).astype(BF16))


def kernel(x, c, norm_g, w_ada, b_ada, rw_mu, rw_w_rkv, rw_w0, rw_w1, rw_w2, rw_a0, rw_a1, rw_a2, rw_g1, rw_g2, rw_k_k, rw_k_a, rw_r_k, rw_lnx_g, rw_lnx_b, rw_w_o, ffn_w_gate, ffn_w_up, ffn_w_down, kv_norm_g, kv_w_ada, kv_b_ada, kv_w_k, kv_w_v, k_norm_g, mb_w_q, mb_q_norm_g, mb_w_o):
    depth = norm_g.shape[0]
    d = x.shape[-1]
    assert depth == 2 and rw_mu.shape[0] == 1 and mb_w_q.shape[0] == 1 and d == D_MODEL

    mods = _ada(c, w_ada.reshape(2 * depth, d, 3 * d), b_ada.reshape(2 * depth, 1, 3 * d))
    kv_mods = _ada(c, kv_w_ada[None], kv_b_ada[None, None])[0]
    part = lambda m, j: m[:, None, j * d:(j + 1) * d]
    vec = lambda a: a.reshape(1, d)

    m = mods[0]
    w1, w2 = _pad_rank(rw_w1[0], rw_w2[0])
    a1, a2 = _pad_rank(rw_a1[0], rw_a2[0])
    g1, g2 = _pad_rank(rw_g1[0], rw_g2[0])
    r, k, v, ld, kkn, kb, g = _rwkv_proj(
        x, part(m, 0), part(m, 1), vec(norm_g[0, 0]), rw_mu[0], rw_w_rkv[0].astype(BF16),
        vec(rw_w0[0]), w1, w2, vec(rw_a0[0]), a1, a2, g1, g2, vec(rw_k_k[0]), vec(rw_k_a[0]))
    y = _rwkv_scan(r, k, v, ld, kkn, kb)
    x = _rwkv_out(y, r, k, v, g, x, part(m, 2), vec(rw_lnx_g[0]), vec(rw_lnx_b[0]),
                  vec(rw_r_k[0]), rw_w_o[0].astype(BF16))
    m = mods[1]
    x = _ffn(x, part(m, 0), part(m, 1), part(m, 2), vec(norm_g[0, 1]),
             ffn_w_gate[0].astype(BF16), ffn_w_up[0].astype(BF16), ffn_w_down[0].astype(BF16))

    m = mods[2]
    q, kx, vx, km = _moba_qkv(
        x, part(m, 0), part(m, 1), vec(norm_g[1, 0]), part(kv_mods, 0), part(kv_mods, 1),
        vec(kv_norm_g), mb_w_q[0].astype(BF16), kv_w_k.astype(BF16), kv_w_v.astype(BF16),
        vec(jnp.tile(mb_q_norm_g[0], N_HEADS)), vec(jnp.tile(k_norm_g, N_HEADS)))
    o = _moba_attn(q, kx, vx, _moba_select(q, km.reshape(x.shape[0], -1, d)))
    x = _proj_res(o, x, part(m, 2), mb_w_o[0].astype(BF16))
    m = mods[3]
    x = _ffn(x, part(m, 0), part(m, 1), part(m, 2), vec(norm_g[1, 1]),
             ffn_w_gate[1].astype(BF16), ffn_w_up[1].astype(BF16), ffn_w_down[1].astype(BF16))
    return x
```

```python
import functools

import jax
import jax.numpy as jnp
from jax import lax
from jax.experimental import pallas as pl
from jax.experimental.pallas import tpu as pltpu

F32 = jnp.float32
BF16 = jnp.bfloat16
HIGHEST = lax.Precision.HIGHEST

D_MODEL = 1024
HEAD_DIM = 64
N_HEADS = D_MODEL // HEAD_DIM
MOBA_BLOCK = 256
MOBA_TOPK = 3
NORM_EPS = 1e-6
GN_EPS = 64e-5
L2_EPS = 1e-12
LOG2_E = 1.4426950408889634

LANES = 128
MXU_WIDTH = 256
HEADS_PER_LANE_GROUP = LANES // HEAD_DIM
VMEM_LIMIT_BYTES = 56 * 1024 * 1024
SCAN_CHUNK = 64
ROW_TILE = 256
SCAN_TIME_BLOCK = 512
SCAN_GROUPS_PER_STEP = 4
ATTN_GROUPS_PER_STEP = 1
SUM_ROWS = 16
SCAN_UNROLL = 2


def _cparams(*sem):
    return pltpu.CompilerParams(dimension_semantics=sem, vmem_limit_bytes=VMEM_LIMIT_BYTES)


def _dot(a, b):
    return jnp.dot(a, b, preferred_element_type=F32)


def _dot_nt(a, b, precision=None):
    return lax.dot_general(a, b, (((1,), (1,)), ((), ())), precision=precision,
                           preferred_element_type=F32)


def _dot_tn(a, b):
    return lax.dot_general(a, b, (((0,), (0,)), ((), ())), preferred_element_type=F32)


def _head_ones(width):
    r = lax.broadcasted_iota(jnp.int32, (width, width), 0) // HEAD_DIM
    c = lax.broadcasted_iota(jnp.int32, (width, width), 1) // HEAD_DIM
    return (r == c).astype(BF16)


def _split_bf16(x, terms):
    parts = []
    for _ in range(terms - 1):
        hi = x.astype(BF16)
        parts.append(hi)
        x = x - hi.astype(F32)
    return parts + [x.astype(BF16)]


def _head_sum(x):
    xb = x.astype(BF16)
    ones = _head_ones(MXU_WIDTH)
    parts = [_dot(xb[:, j * MXU_WIDTH:(j + 1) * MXU_WIDTH], ones)
             for j in range(x.shape[1] // MXU_WIDTH)]
    return jnp.concatenate(parts, axis=1)


def _ada_ln(x, g, shift, scale):
    inv = lax.rsqrt(jnp.mean(x * x, axis=-1, keepdims=True) + NORM_EPS)
    return (x * inv * g) * (1.0 + scale) + shift


def _ada_kernel(c_ref, w_ref, b_ref, o_ref):
    c = c_ref[...]
    s = c * jax.nn.sigmoid(c)
    o_ref[...] = jnp.dot(s, w_ref[...], precision=HIGHEST, preferred_element_type=F32) + b_ref[...]


def _ada(c, w, b):
    n, d, md = w.shape
    bsz = c.shape[0]
    return pl.pallas_call(
        _ada_kernel,
        out_shape=jax.ShapeDtypeStruct((n, bsz, md), F32),
        grid=(n, md // d),
        in_specs=[pl.BlockSpec((bsz, d), lambda i, j: (0, 0)),
                  pl.BlockSpec((None, d, d), lambda i, j: (i, 0, j)),
                  pl.BlockSpec((None, 1, d), lambda i, j: (i, 0, j))],
        out_specs=pl.BlockSpec((None, bsz, d), lambda i, j: (i, 0, j)),
        compiler_params=_cparams("parallel", "parallel"),
        name="ada",
    )(c, w, b)


def _rwkv_proj_kernel(x_ref, shift_ref, scale_ref, ng_ref, mu_ref, wrkv_ref, w0_ref, w1_ref,
                      w2_ref, a0_ref, a1_ref, a2_ref, g1_ref, g2_ref, kk_ref, ka_ref,
                      r_out, k_out, v_out, ld_out, kkn_out, kb_out, g_out, carry_ref):
    tm = x_ref.shape[0]
    h = _ada_ln(x_ref[...], ng_ref[...], shift_ref[...], scale_ref[...])
    first = pl.program_id(1) == 0
    prev_last = jnp.where(first, 0.0, carry_ref[7:8, :])
    row = lax.broadcasted_iota(jnp.int32, h.shape, 0)
    h_prev = jnp.where(row == 0, prev_last, pltpu.roll(h, 1, axis=0))
    carry_ref[...] = h[tm - 8:tm, :]
    d = h_prev - h
    mix = lambda i: (h + d * mu_ref[i:i + 1, :]).astype(BF16)

    r = _dot(mix(0), wrkv_ref[0])
    k = _dot(mix(1), wrkv_ref[1])
    v = _dot(mix(2), wrkv_ref[2])
    zw = w0_ref[...] + _dot(jnp.tanh(_dot(mix(3), w1_ref[...])).astype(BF16), w2_ref[...])
    a = jax.nn.sigmoid(a0_ref[...] + _dot(_dot(mix(4), a1_ref[...]).astype(BF16), a2_ref[...]))
    g = _dot(jax.nn.sigmoid(_dot(mix(5), g1_ref[...])).astype(BF16), g2_ref[...])

    ld_out[...] = -jnp.exp(F32(-0.5)) * jax.nn.sigmoid(zw)
    kk = k * kk_ref[...]
    nrm = jnp.maximum(jnp.sqrt(_head_sum(kk * kk)), L2_EPS)
    kk = kk / nrm
    r_out[...] = r.astype(r_out.dtype)
    k_out[...] = (k * (1.0 + (a - 1.0) * ka_ref[...])).astype(k_out.dtype)
    v_out[...] = v.astype(v_out.dtype)
    kkn_out[...] = kk.astype(kkn_out.dtype)
    kb_out[...] = (kk * a).astype(kb_out.dtype)
    g_out[...] = g.astype(g_out.dtype)


def _rwkv_proj(x, shift, scale, ng, mu, wrkv, w0, w1, w2, a0, a1, a2, g1, g2, k_k, k_a):
    bsz, t, d = x.shape
    tm = ROW_TILE
    row = pl.BlockSpec((None, tm, d), lambda b, i: (b, i, 0))
    mod = pl.BlockSpec((None, 1, d), lambda b, i: (b, 0, 0))
    vec = pl.BlockSpec((1, d), lambda b, i: (0, 0))
    full = lambda a: pl.BlockSpec(a.shape, lambda b, i: (0,) * a.ndim)
    out = lambda dt: jax.ShapeDtypeStruct((bsz, t, d), dt)
    return pl.pallas_call(
        _rwkv_proj_kernel,
        out_shape=(out(BF16), out(BF16), out(BF16), out(F32), out(BF16), out(BF16), out(BF16)),
        grid=(bsz, t // tm),
        in_specs=[row, mod, mod, vec, full(mu), full(wrkv), vec, full(w1), full(w2), vec,
                  full(a1), full(a2), full(g1), full(g2), vec, vec],
        out_specs=(row,) * 7,
        scratch_shapes=[pltpu.VMEM((8, d), F32)],
        compiler_params=_cparams("arbitrary", "arbitrary"),
        name="rwkv_proj",
    )(x, shift, scale, ng, mu, wrkv, w0, w1, w2, a0, a1, a2, g1, g2, k_k, k_a)


def _rwkv_scan_kernel(r_ref, k_ref, v_ref, ld_ref, kk_ref, kb_ref, y_ref,
                      s_ref, ry_ref, y1_ref, rr_ref, q_ref, g_ref):
    c = SCAN_CHUNK
    n_chunks = r_ref.shape[0] // c
    n_groups = r_ref.shape[1] // LANES
    two_c = HEADS_PER_LANE_GROUP * c

    @pl.when(pl.program_id(2) == 0)
    def _():
        s_ref[...] = jnp.zeros_like(s_ref)

    ri = lax.broadcasted_iota(jnp.int32, (two_c, two_c), 0)
    ci = lax.broadcasted_iota(jnp.int32, (two_c, two_c), 1)
    same_head = (ri // c) == (ci // c)
    strict = jnp.logical_and(same_head, ri > ci)
    incl = jnp.logical_and(same_head, ri >= ci)
    eye = (ri == ci).astype(F32)
    ti = lax.broadcasted_iota(jnp.int32, (c, c), 0)
    tj = lax.broadcasted_iota(jnp.int32, (c, c), 1)
    tri3 = jnp.concatenate([(ti >= tj).astype(BF16)] * 3, axis=1)
    cumsum = lambda z: _dot(tri3, jnp.concatenate(_split_bf16(z, 3), axis=0))
    head0 = lax.broadcasted_iota(jnp.int32, (c, LANES), 1) < HEAD_DIM

    def stack(z):
        return jnp.concatenate([jnp.where(head0, z, 0.0), jnp.where(head0, 0.0, z)], axis=0)

    def prepare(chains):
        each = lambda f, *cols: [f(*args) for args in zip(*cols)]
        rows = [pl.ds(pl.multiple_of(ic * c, c), c) for ic, _ in chains]
        lanes = [slice(gi * LANES, (gi + 1) * LANES) for _, gi in chains]
        load = lambda ref: each(lambda rw, ln: ref[rw, ln].astype(F32), rows, lanes)
        ld = load(ld_ref)
        cum = each(cumsum, ld)
        tot = each(lambda z: z[c - 1:c, :], cum)
        kb, kx = load(kb_ref), load(k_ref)
        g_inv = each(lambda z: jnp.exp(-z), cum)
        g_rest = each(lambda t_, z: jnp.exp(t_ - z), tot, cum)
        at = each(lambda kn, z, l_: stack(-kn * jnp.exp(z - l_)), load(kk_ref), cum, ld)
        rt = each(lambda r_, z: stack(r_ * jnp.exp(z)), load(r_ref), cum)
        bh = each(lambda x_, g_: stack(x_ * g_), kb, g_inv)
        kh = each(lambda x_, g_: stack(x_ * g_), kx, g_inv)
        bc = each(lambda x_, g_: stack(x_ * g_), kb, g_rest)
        kc = each(lambda x_, g_: stack(x_ * g_), kx, g_rest)
        vs = each(stack, load(v_ref))
        bf = lambda zs: [z.astype(BF16) for z in zs]
        at_b, bc_b, vs_b = bf(at), bf(bc), bf(vs)
        qa = each(lambda a_, r_: jnp.concatenate([a_, r_.astype(BF16)], axis=0), at_b, rt)
        mb = each(_dot_nt, qa, bf(bh))
        mk = each(_dot_nt, qa, bf(kh))
        a_rb = bf(each(lambda z: jnp.where(incl, z[two_c:], 0.0), mb))
        a_ak = bf(each(lambda z: jnp.where(strict, z[:two_c], 0.0), mk))
        a_rk = bf(each(lambda z: jnp.where(incl, z[two_c:], 0.0), mk))
        pw = each(lambda z: jnp.where(strict, z[:two_c], 0.0), mb)
        t_inv = each(lambda z: eye + z, pw)
        for _ in range(5):
            pw_b = bf(pw)
            pw = each(_dot, pw_b, pw_b)
            t_inv = each(lambda t_, p_: t_ + _dot(t_.astype(BF16), p_.astype(BF16)), t_inv, pw)
        t_inv = bf(t_inv)
        w1 = bf(each(_dot, t_inv, at_b))
        av = bf(each(_dot, a_ak, vs_b))
        u0 = bf(each(_dot, t_inv, av))
        ry = each(lambda r_, a_, w_: r_ + _dot(a_, w_), rt, a_rb, w1)
        y1 = each(lambda a_, u_, b_, v_: _dot(a_, u_) + _dot(b_, v_), a_rb, u0, a_rk, vs_b)
        rr = each(_dot_tn, w1, bc_b)
        qq = each(lambda u_, b_, v_, k_: _dot_tn(u_, b_) + _dot_tn(v_, k_), u0, bc_b, vs_b, bf(kc))
        for j, (ic, gi) in enumerate(chains):
            ry_ref[gi, ic] = ry[j].astype(BF16)
            y1_ref[gi, ic] = y1[j][:c] + y1[j][c:]
            rr_ref[gi, ic] = rr[j].astype(BF16)
            q_ref[gi, ic] = qq[j]
            g_ref[gi, ic] = jnp.broadcast_to(jnp.exp(tot[j]), (8, LANES))

    def prepare_group(ig, carry):
        prepare([(ig * SCAN_UNROLL + u, gi) for u in range(SCAN_UNROLL) for gi in range(n_groups)])
        return carry

    lax.fori_loop(0, n_chunks // SCAN_UNROLL, prepare_group, 0)

    def advance(ic, states):
        rows = pl.ds(pl.multiple_of(ic * c, c), c)
        groups = range(n_groups)
        s_b = [states[gi].astype(BF16) for gi in groups]
        sr = [_dot(s_b[gi], rr_ref[gi, ic]) for gi in groups]
        ys = [_dot_nt(ry_ref[gi, ic], s_b[gi]) for gi in groups]
        for gi in groups:
            y = ys[gi][:c] + ys[gi][c:] + y1_ref[gi, ic]
            y_ref[rows, gi * LANES:(gi + 1) * LANES] = y.astype(y_ref.dtype)
        return tuple(states[gi] * g_ref[gi, ic, 0:1, :] + sr[gi] + q_ref[gi, ic] for gi in groups)

    final = lax.fori_loop(0, n_chunks, advance, tuple(s_ref[gi] for gi in range(n_groups)))
    for gi in range(n_groups):
        s_ref[gi] = final[gi]


def _rwkv_scan(r, k, v, ld, kkn, kb):
    bsz, t, d = r.shape
    tb = SCAN_TIME_BLOCK
    n_groups = SCAN_GROUPS_PER_STEP
    n_chunks = tb // SCAN_CHUNK
    two_c = HEADS_PER_LANE_GROUP * SCAN_CHUNK
    blk = pl.BlockSpec((None, tb, n_groups * LANES), lambda b, hp, i: (b, i, hp))
    mats = lambda dt: pltpu.VMEM((n_groups, n_chunks, two_c, LANES), dt)
    return pl.pallas_call(
        _rwkv_scan_kernel,
        out_shape=jax.ShapeDtypeStruct((bsz, t, d), BF16),
        grid=(bsz, d // (n_groups * LANES), t // tb),
        in_specs=[blk] * 6,
        out_specs=blk,
        scratch_shapes=[pltpu.VMEM((n_groups, two_c, LANES), F32), mats(BF16),
                        pltpu.VMEM((n_groups, n_chunks, SCAN_CHUNK, LANES), F32), mats(BF16),
                        mats(F32),
                        pltpu.VMEM((n_groups, n_chunks, 8, LANES), F32)],
        compiler_params=_cparams("parallel", "parallel", "arbitrary"),
        name="rwkv_scan",
    )(r, k, v, ld, kkn, kb)


def _rwkv_out_kernel(y_ref, r_ref, k_ref, v_ref, g_ref, x_ref, gate_ref, lg_ref, lb_ref, rk_ref,
                     wo_ref, o_ref):
    f32 = lambda ref: ref[...].astype(F32)
    y = f32(y_ref)
    mean = _head_sum(y) * (1.0 / HEAD_DIM)
    dy = y - mean
    var = _head_sum(dy * dy) * (1.0 / HEAD_DIM)
    yn = dy * lax.rsqrt(var + GN_EPS) * lg_ref[...] + lb_ref[...]
    bonus = _head_sum(f32(r_ref) * f32(k_ref) * rk_ref[...]) * f32(v_ref)
    mix = _dot(((yn + bonus) * f32(g_ref)).astype(BF16), wo_ref[...])
    o_ref[...] = x_ref[...] + gate_ref[...] * mix


def _rwkv_out(y, r, k, v, g, x, gate, lnx_g, lnx_b, r_k, wo):
    bsz, t, d = x.shape
    tm = ROW_TILE
    row = pl.BlockSpec((None, tm, d), lambda b, i: (b, i, 0))
    mod = pl.BlockSpec((None, 1, d), lambda b, i: (b, 0, 0))
    vec = pl.BlockSpec((1, d), lambda b, i: (0, 0))
    return pl.pallas_call(
        _rwkv_out_kernel,
        out_shape=jax.ShapeDtypeStruct((bsz, t, d), F32),
        grid=(bsz, t // tm),
        in_specs=[row] * 6 + [mod, vec, vec, vec, pl.BlockSpec((d, d), lambda b, i: (0, 0))],
        out_specs=row,
        compiler_params=_cparams("parallel", "parallel"),
        name="rwkv_out",
    )(y, r, k, v, g, x, gate, lnx_g, lnx_b, r_k, wo)


def _ffn_kernel(x_ref, shift_ref, scale_ref, gate_ref, ng_ref, wg_ref, wu_ref, wd_ref, o_ref):
    x = x_ref[...]
    h = _ada_ln(x, ng_ref[...], shift_ref[...], scale_ref[...]).astype(BF16)
    gt = _dot(h, wg_ref[...])
    up = _dot(h, wu_ref[...])
    act = (gt * jax.nn.sigmoid(gt) * up).astype(BF16)
    o_ref[...] = x + gate_ref[...] * _dot(act, wd_ref[...])


def _ffn(x, shift, scale, gate, ng, wg, wu, wd):
    bsz, t, d = x.shape
    f = wg.shape[1]
    tm = ROW_TILE
    row = pl.BlockSpec((None, tm, d), lambda b, i: (b, i, 0))
    mod = pl.BlockSpec((None, 1, d), lambda b, i: (b, 0, 0))
    vec = pl.BlockSpec((1, d), lambda b, i: (0, 0))
    return pl.pallas_call(
        _ffn_kernel,
        out_shape=jax.ShapeDtypeStruct((bsz, t, d), F32),
        grid=(bsz, t // tm),
        in_specs=[row, mod, mod, mod, vec,
                  pl.BlockSpec((d, f), lambda b, i: (0, 0)),
                  pl.BlockSpec((d, f), lambda b, i: (0, 0)),
                  pl.BlockSpec((f, d), lambda b, i: (0, 0))],
        out_specs=row,
        compiler_params=_cparams("parallel", "parallel"),
        name="ffn",
    )(x, shift, scale, gate, ng, wg, wu, wd)


def _head_rms(z, g):
    ms = _head_sum(z * z) * (1.0 / HEAD_DIM)
    return z * lax.rsqrt(ms + NORM_EPS) * g


def _moba_qkv_kernel(x_ref, qshift_ref, qscale_ref, qng_ref, kshift_ref, kscale_ref, kng_ref,
                     wq_ref, wk_ref, wv_ref, qg_ref, kg_ref, qt_out, k_out, vt_out, km_out):
    x = x_ref[...]
    xn = x * lax.rsqrt(jnp.mean(x * x, axis=-1, keepdims=True) + NORM_EPS)
    hq = ((xn * qng_ref[...]) * (1.0 + qscale_ref[...]) + qshift_ref[...]).astype(BF16)
    hk = ((xn * kng_ref[...]) * (1.0 + kscale_ref[...]) + kshift_ref[...]).astype(BF16)
    qt_out[...] = _head_rms(_dot(hq, wq_ref[...]), qg_ref[...]).T
    k = _head_rms(_dot(hk, wk_ref[...]), kg_ref[...])
    k_out[...] = k.astype(BF16)
    vt_out[...] = _dot(hk, wv_ref[...]).T.astype(BF16)
    km_out[...] = jnp.mean(k, axis=0, keepdims=True)


def _moba_qkv(x, qshift, qscale, qng, kshift, kscale, kng, wq, wk, wv, qg, kg):
    bsz, t, d = x.shape
    tm = MOBA_BLOCK
    nb = t // tm
    row = pl.BlockSpec((None, tm, d), lambda b, i: (b, i, 0))
    mod = pl.BlockSpec((None, 1, d), lambda b, i: (b, 0, 0))
    vec = pl.BlockSpec((1, d), lambda b, i: (0, 0))
    mat = pl.BlockSpec((d, d), lambda b, i: (0, 0))
    return pl.pallas_call(
        _moba_qkv_kernel,
        out_shape=(jax.ShapeDtypeStruct((bsz, d, t), F32),
                   jax.ShapeDtypeStruct((bsz, t, d), BF16),
                   jax.ShapeDtypeStruct((bsz, nb, d, tm), BF16),
                   jax.ShapeDtypeStruct((bsz * nb, 1, d), F32)),
        grid=(bsz, nb),
        in_specs=[row, mod, mod, vec, mod, mod, vec, mat, mat, mat, vec, vec],
        out_specs=(pl.BlockSpec((None, d, tm), lambda b, i: (b, 0, i)), row,
                   pl.BlockSpec((None, None, d, tm), lambda b, i: (b, i, 0, 0)),
                   pl.BlockSpec((None, 1, d), lambda b, i: (b * nb + i, 0, 0))),
        compiler_params=_cparams("parallel", "parallel"),
        name="moba_qkv",
    )(x, qshift, qscale, qng, kshift, kscale, kng, wq, wk, wv, qg, kg)


def _moba_attn_kernel(qt_ref, k_ref, vt_ref, bias_ref, o_ref, qs_ref, sa_ref, sb_ref, p_ref):
    blk = MOBA_BLOCK
    nb = bias_ref.shape[1]
    qb = pl.program_id(2)
    heads = range(qs_ref.shape[0])
    group = lambda hd: slice(hd // HEADS_PER_LANE_GROUP * LANES, (hd // HEADS_PER_LANE_GROUP + 1) * LANES)
    neg_inf = F32(-jnp.inf)
    chan_head = lax.broadcasted_iota(jnp.int32, (LANES, blk), 0) // HEAD_DIM
    key_i = lax.broadcasted_iota(jnp.int32, (blk, blk), 0)
    qry_i = lax.broadcasted_iota(jnp.int32, (blk, blk), 1)
    causal = key_i <= qry_i
    q_scale = F32(HEAD_DIM ** -0.5 * LOG2_E)
    for hd in heads:
        keep = chan_head == hd % HEADS_PER_LANE_GROUP
        qs_ref[hd] = (jnp.where(keep, qt_ref[group(hd), :], 0.0) * q_scale).astype(BF16)

    def put_scores(n, dst_ref):
        rows = pl.ds(pl.multiple_of(n * blk, blk), blk)
        for hd in heads:
            dst_ref[hd] = _dot(k_ref[rows, group(hd)], qs_ref[hd])

    def bias_rows(n):
        return [bias_ref[hd, pl.ds(n, 1), :] for hd in heads]

    ones_rows = jnp.ones((SUM_ROWS, blk), BF16)

    def weighted_values(n, p, rows):
        vtn = vt_ref[n]
        lhs = [jnp.concatenate([vtn[hd * HEAD_DIM:(hd + 1) * HEAD_DIM, :], ones_rows], axis=0)
               for hd in heads]
        return [jnp.where(rows[hd] == 0.0, _dot(lhs[hd], p[hd]), 0.0) for hd in heads]

    def softmax_step(src_ref, m, rows):
        s = [src_ref[hd] for hd in heads]
        m_new = [jnp.maximum(m[hd], jnp.max(s[hd], axis=0, keepdims=True) + rows[hd]) for hd in heads]
        alpha = [jnp.exp2(m[hd] - m_new[hd]) for hd in heads]
        p = [jnp.exp2(s[hd] - m_new[hd]).astype(BF16) for hd in heads]
        return m_new, alpha, p

    past_of = lambda t: jnp.clip(t - 1, 0, nb - 1)
    values_of = lambda t: jnp.where(t == 0, qb, past_of(t))
    rows_of = lambda t: [jnp.where(t == 0, 0.0, r) for r in bias_rows(past_of(t))]

    def tile_pair(j, carry):
        m, acc, alpha_prev = carry
        a, b = 2 * j, 2 * j + 1
        prev = jnp.maximum(a - 1, 0)
        rows_a, rows_b = rows_of(a), rows_of(b)
        put_scores(past_of(b), sb_ref)
        pv_prev = weighted_values(values_of(prev), [p_ref[hd] for hd in heads], rows_of(prev))
        m, alpha_a, p_a = softmax_step(sa_ref, m, rows_a)
        put_scores(past_of(b + 1), sa_ref)
        pv_a = weighted_values(values_of(a), p_a, rows_a)
        m, alpha_b, p_b = softmax_step(sb_ref, m, rows_b)
        for hd in heads:
            p_ref[hd] = p_b[hd]
        acc = [alpha_a[hd] * (alpha_prev[hd] * acc[hd] + pv_prev[hd]) + pv_a[hd] for hd in heads]
        return m, acc, alpha_b

    own = pl.ds(pl.multiple_of(qb * blk, blk), blk)
    for hd in heads:
        sa_ref[hd] = jnp.where(causal, _dot(k_ref[own, group(hd)], qs_ref[hd]), neg_inf)
    p_ref[...] = jnp.zeros_like(p_ref)
    init = ([jnp.full((1, blk), neg_inf, F32) for _ in heads],
            [jnp.zeros((HEAD_DIM + SUM_ROWS, blk), F32) for _ in heads],
            [jnp.ones((1, blk), F32) for _ in heads])
    n_pairs = (qb + 2) // 2
    m, acc, alpha_prev = lax.fori_loop(0, n_pairs, tile_pair, init)
    last = 2 * n_pairs - 1
    pv_last = weighted_values(values_of(last), [p_ref[hd] for hd in heads], rows_of(last))
    acc = [alpha_prev[hd] * acc[hd] + pv_last[hd] for hd in heads]
    ot = jnp.concatenate([acc[hd][:HEAD_DIM] / acc[hd][HEAD_DIM:HEAD_DIM + 1] for hd in heads], axis=0)
    o_ref[...] = ot.T


def _moba_attn(qt, k, vt, bias):
    bsz, d, t = qt.shape
    nb = t // MOBA_BLOCK
    width = ATTN_GROUPS_PER_STEP * LANES
    hpg = ATTN_GROUPS_PER_STEP * HEADS_PER_LANE_GROUP
    return pl.pallas_call(
        _moba_attn_kernel,
        out_shape=jax.ShapeDtypeStruct((bsz, t, d), F32),
        grid=(bsz, d // width, nb),
        in_specs=[pl.BlockSpec((None, width, MOBA_BLOCK), lambda b, hp, i: (b, hp, i)),
                  pl.BlockSpec((None, t, width), lambda b, hp, i: (b, 0, hp)),
                  pl.BlockSpec((None, nb, width, MOBA_BLOCK), lambda b, hp, i: (b, 0, hp, 0)),
                  pl.BlockSpec((None, None, hpg, nb, MOBA_BLOCK), lambda b, hp, i: (b, i, hp, 0, 0))],
        out_specs=pl.BlockSpec((None, MOBA_BLOCK, width), lambda b, hp, i: (b, i, hp)),
        scratch_shapes=[pltpu.VMEM((hpg, LANES, MOBA_BLOCK), BF16),
                        pltpu.VMEM((hpg, MOBA_BLOCK, MOBA_BLOCK), F32),
                        pltpu.VMEM((hpg, MOBA_BLOCK, MOBA_BLOCK), F32),
                        pltpu.VMEM((hpg, MOBA_BLOCK, MOBA_BLOCK), BF16)],
        compiler_params=_cparams("parallel", "parallel", "arbitrary"),
        name="moba_attn",
    )(qt, k, vt, bias)


def _moba_select_kernel(qt_ref, km_ref, bias_ref):
    nb = km_ref.shape[0]
    blk = qt_ref.shape[1]
    qb = pl.program_id(1)
    neg_inf = F32(-jnp.inf)
    rows = HEADS_PER_LANE_GROUP * nb
    row_head = lax.broadcasted_iota(jnp.int32, (rows, LANES), 0) // nb
    chan_head = lax.broadcasted_iota(jnp.int32, (rows, LANES), 1) // HEAD_DIM
    gates = []
    for g in range(qt_ref.shape[0] // LANES):
        km = km_ref[:, g * LANES:(g + 1) * LANES]
        km2 = jnp.where(row_head == chan_head, jnp.concatenate([km] * HEADS_PER_LANE_GROUP, axis=0), 0.0)
        gates.append(jnp.dot(km2, qt_ref[g * LANES:(g + 1) * LANES, :], precision=HIGHEST,
                             preferred_element_type=F32))
    gate = jnp.concatenate(gates, axis=0).reshape(N_HEADS, nb, blk)
    blk_iota = lax.broadcasted_iota(jnp.int32, gate.shape, 1)
    valid = blk_iota < qb
    gate = jnp.where(valid, gate, neg_inf)
    sel = jnp.zeros(gate.shape, jnp.bool_)
    for _ in range(MOBA_TOPK):
        mx = jnp.max(gate, axis=1, keepdims=True)
        idx = jnp.min(jnp.where(gate == mx, blk_iota, nb), axis=1, keepdims=True)
        hit = blk_iota == idx
        sel = jnp.logical_or(sel, hit)
        gate = jnp.where(hit, neg_inf, gate)
    bias_ref[...] = jnp.where(jnp.logical_and(sel, valid), 0.0, neg_inf)


def _moba_select(qt, km):
    bsz, d, t = qt.shape
    nb = km.shape[1]
    return pl.pallas_call(
        _moba_select_kernel,
        out_shape=jax.ShapeDtypeStruct((bsz, nb, N_HEADS, nb, MOBA_BLOCK), F32),
        grid=(bsz, nb),
        in_specs=[pl.BlockSpec((None, d, MOBA_BLOCK), lambda b, i: (b, 0, i)),
                  pl.BlockSpec((None, nb, d), lambda b, i: (b, 0, 0))],
        out_specs=pl.BlockSpec((None, None, N_HEADS, nb, MOBA_BLOCK), lambda b, i: (b, i, 0, 0, 0)),
        compiler_params=_cparams("parallel", "parallel"),
        name="moba_select",
    )(qt, km)


def _proj_res_kernel(o_ref, x_ref, gate_ref, w_ref, out_ref):
    out_ref[...] = x_ref[...] + gate_ref[...] * _dot(o_ref[...].astype(BF16), w_ref[...])


def _proj_res(o, x, gate, w):
    bsz, t, d = x.shape
    tm = ROW_TILE
    row = pl.BlockSpec((None, tm, d), lambda b, i: (b, i, 0))
    return pl.pallas_call(
        _proj_res_kernel,
        out_shape=jax.ShapeDtypeStruct((bsz, t, d), F32),
        grid=(bsz, t // tm),
        in_specs=[row, row, pl.BlockSpec((None, 1, d), lambda b, i: (b, 0, 0)),
                  pl.BlockSpec((d, d), lambda b, i: (0, 0))],
        out_specs=row,
        compiler_params=_cparams("parallel", "parallel"),
        name="proj_res",
    )(o, x, gate, w)


def _pad_rank(w_down, w_up):
    rank = w_down.shape[1]
    pad = -rank % LANES
    return (jnp.pad(w_down, ((0, 0), (0, pad))).astype(BF16),
            jnp.pad(w_up, ((0, pad), (0, 0))).astype(BF16))


def kernel(x, c, norm_g, w_ada, b_ada, rw_mu, rw_w_rkv, rw_w0, rw_w1, rw_w2, rw_a0, rw_a1, rw_a2, rw_g1, rw_g2, rw_k_k, rw_k_a, rw_r_k, rw_lnx_g, rw_lnx_b, rw_w_o, ffn_w_gate, ffn_w_up, ffn_w_down, kv_norm_g, kv_w_ada, kv_b_ada, kv_w_k, kv_w_v, k_norm_g, mb_w_q, mb_q_norm_g, mb_w_o):
    depth = norm_g.shape[0]
    d = x.shape[-1]
    assert depth == 2 and rw_mu.shape[0] == 1 and mb_w_q.shape[0] == 1 and d == D_MODEL

    mods = _ada(c, w_ada.reshape(2 * depth, d, 3 * d), b_ada.reshape(2 * depth, 1, 3 * d))
    kv_mods = _ada(c, kv_w_ada[None], kv_b_ada[None, None])[0]
    part = lambda m, j: m[:, None, j * d:(j + 1) * d]
    vec = lambda a: a.reshape(1, d)

    m = mods[0]
    w1, w2 = _pad_rank(rw_w1[0], rw_w2[0])
    a1, a2 = _pad_rank(rw_a1[0], rw_a2[0])
    g1, g2 = _pad_rank(rw_g1[0], rw_g2[0])
    r, k, v, ld, kkn, kb, g = _rwkv_proj(
        x, part(m, 0), part(m, 1), vec(norm_g[0, 0]), rw_mu[0], rw_w_rkv[0].astype(BF16),
        vec(rw_w0[0]), w1, w2, vec(rw_a0[0]), a1, a2, g1, g2, vec(rw_k_k[0]), vec(rw_k_a[0]))
    y = _rwkv_scan(r, k, v, ld, kkn, kb)
    x = _rwkv_out(y, r, k, v, g, x, part(m, 2), vec(rw_lnx_g[0]), vec(rw_lnx_b[0]),
                  vec(rw_r_k[0]), rw_w_o[0].astype(BF16))
    m = mods[1]
    x = _ffn(x, part(m, 0), part(m, 1), part(m, 2), vec(norm_g[0, 1]),
             ffn_w_gate[0].astype(BF16), ffn_w_up[0].astype(BF16), ffn_w_down[0].astype(BF16))

    m = mods[2]
    q, kx, vx, km = _moba_qkv(
        x, part(m, 0), part(m, 1), vec(norm_g[1, 0]), part(kv_mods, 0), part(kv_mods, 1),
        vec(kv_norm_g), mb_w_q[0].astype(BF16), kv_w_k.astype(BF16), kv_w_v.astype(BF16),
        vec(jnp.tile(mb_q_norm_g[0], N_HEADS)), vec(jnp.tile(k_norm_g, N_HEADS)))
    o = _moba_attn(q, kx, vx, _moba_select(q, km.reshape(x.shape[0], -1, d)))
    x = _proj_res(o, x, part(m, 2), mb_w_o[0].astype(BF16))
    m = mods[3]
    x = _ffn(x, part(m, 0), part(m, 1), part(m, 2), vec(norm_g[1, 1]),
             ffn_w_gate[1].astype(BF16), ffn_w_up[1].astype(BF16), ffn_w_down[1].astype(BF16))
    return x
```

```python
import jax
import jax.numpy as jnp
from jax import lax
from jax.experimental import pallas as pl
from jax.experimental.pallas import tpu as pltpu

F32 = jnp.float32
BF16 = jnp.bfloat16
HIGHEST = lax.Precision.HIGHEST

D_MODEL = 1024
HEAD_DIM = 64
N_HEADS = D_MODEL // HEAD_DIM
MOBA_BLOCK = 256
MOBA_TOPK = 3
NORM_EPS = 1e-6
GN_EPS = 64e-5
L2_EPS = 1e-12
LOG2_E = 1.4426950408889634

LANES = 128
MXU_WIDTH = 256
HEADS_PER_LANE_GROUP = LANES // HEAD_DIM
VMEM_LIMIT_BYTES = 56 * 1024 * 1024
SCAN_CHUNK = 64
ROW_TILE = 256
SCAN_TIME_BLOCK = 512
SCAN_GROUPS_PER_STEP = 4
ATTN_GROUPS_PER_STEP = 1
SUM_ROWS = 16
SCAN_UNROLL = 2


def _cparams(*sem):
    return pltpu.CompilerParams(dimension_semantics=sem, vmem_limit_bytes=VMEM_LIMIT_BYTES)


def _dot(a, b):
    return jnp.dot(a, b, preferred_element_type=F32)


def _dot_nt(a, b, precision=None):
    return lax.dot_general(a, b, (((1,), (1,)), ((), ())), precision=precision,
                           preferred_element_type=F32)


def _dot_tn(a, b):
    return lax.dot_general(a, b, (((0,), (0,)), ((), ())), preferred_element_type=F32)


def _head_ones(width):
    r = lax.broadcasted_iota(jnp.int32, (width, width), 0) // HEAD_DIM
    c = lax.broadcasted_iota(jnp.int32, (width, width), 1) // HEAD_DIM
    return (r == c).astype(BF16)


def _split_bf16(x, terms):
    parts = []
    for _ in range(terms - 1):
        hi = x.astype(BF16)
        parts.append(hi)
        x = x - hi.astype(F32)
    return parts + [x.astype(BF16)]


def _head_sum(x):
    xb = x.astype(BF16)
    ones = _head_ones(MXU_WIDTH)
    parts = [_dot(xb[:, j * MXU_WIDTH:(j + 1) * MXU_WIDTH], ones)
             for j in range(x.shape[1] // MXU_WIDTH)]
    return jnp.concatenate(parts, axis=1)


def _ada_ln(x, g, shift, scale):
    inv = lax.rsqrt(jnp.mean(x * x, axis=-1, keepdims=True) + NORM_EPS)
    return (x * inv * g) * (1.0 + scale) + shift


def _ada_kernel(c_ref, w_ref, b_ref, o_ref):
    c = c_ref[...]
    s = c * jax.nn.sigmoid(c)
    o_ref[...] = jnp.dot(s, w_ref[...], precision=HIGHEST, preferred_element_type=F32) + b_ref[...]


def _ada(c, w, b):
    n, d, md = w.shape
    bsz = c.shape[0]
    return pl.pallas_call(
        _ada_kernel,
        out_shape=jax.ShapeDtypeStruct((n, bsz, md), F32),
        grid=(n, md // d),
        in_specs=[pl.BlockSpec((bsz, d), lambda i, j: (0, 0)),
                  pl.BlockSpec((None, d, d), lambda i, j: (i, 0, j)),
                  pl.BlockSpec((None, 1, d), lambda i, j: (i, 0, j))],
        out_specs=pl.BlockSpec((None, bsz, d), lambda i, j: (i, 0, j)),
        compiler_params=_cparams("parallel", "parallel"),
        name="ada",
    )(c, w, b)


def _rwkv_proj_kernel(x_ref, shift_ref, scale_ref, ng_ref, mu_ref, wrkv_ref, w0_ref, w1_ref,
                      w2_ref, a0_ref, a1_ref, a2_ref, g1_ref, g2_ref, kk_ref, ka_ref,
                      r_out, k_out, v_out, ld_out, kkn_out, kb_out, g_out, carry_ref):
    tm = x_ref.shape[0]
    h = _ada_ln(x_ref[...], ng_ref[...], shift_ref[...], scale_ref[...])
    first = pl.program_id(1) == 0
    prev_last = jnp.where(first, 0.0, carry_ref[7:8, :])
    row = lax.broadcasted_iota(jnp.int32, h.shape, 0)
    h_prev = jnp.where(row == 0, prev_last, pltpu.roll(h, 1, axis=0))
    carry_ref[...] = h[tm - 8:tm, :]
    d = h_prev - h
    mix = lambda i: (h + d * mu_ref[i:i + 1, :]).astype(BF16)

    r = _dot(mix(0), wrkv_ref[0])
    k = _dot(mix(1), wrkv_ref[1])
    v = _dot(mix(2), wrkv_ref[2])
    zw = w0_ref[...] + _dot(jnp.tanh(_dot(mix(3), w1_ref[...])).astype(BF16), w2_ref[...])
    a = jax.nn.sigmoid(a0_ref[...] + _dot(_dot(mix(4), a1_ref[...]).astype(BF16), a2_ref[...]))
    g = _dot(jax.nn.sigmoid(_dot(mix(5), g1_ref[...])).astype(BF16), g2_ref[...])

    ld_out[...] = -jnp.exp(F32(-0.5)) * jax.nn.sigmoid(zw)
    kk = k * kk_ref[...]
    nrm = jnp.maximum(jnp.sqrt(_head_sum(kk * kk)), L2_EPS)
    kk = kk / nrm
    r_out[...] = r.astype(r_out.dtype)
    k_out[...] = (k * (1.0 + (a - 1.0) * ka_ref[...])).astype(k_out.dtype)
    v_out[...] = v.astype(v_out.dtype)
    kkn_out[...] = kk.astype(kkn_out.dtype)
    kb_out[...] = (kk * a).astype(kb_out.dtype)
    g_out[...] = g.astype(g_out.dtype)


def _rwkv_proj(x, shift, scale, ng, mu, wrkv, w0, w1, w2, a0, a1, a2, g1, g2, k_k, k_a):
    bsz, t, d = x.shape
    tm = ROW_TILE
    row = pl.BlockSpec((None, tm, d), lambda b, i: (b, i, 0))
    mod = pl.BlockSpec((None, 1, d), lambda b, i: (b, 0, 0))
    vec = pl.BlockSpec((1, d), lambda b, i: (0, 0))
    full = lambda a: pl.BlockSpec(a.shape, lambda b, i: (0,) * a.ndim)
    out = lambda dt: jax.ShapeDtypeStruct((bsz, t, d), dt)
    return pl.pallas_call(
        _rwkv_proj_kernel,
        out_shape=(out(BF16), out(BF16), out(BF16), out(F32), out(BF16), out(BF16), out(BF16)),
        grid=(bsz, t // tm),
        in_specs=[row, mod, mod, vec, full(mu), full(wrkv), vec, full(w1), full(w2), vec,
                  full(a1), full(a2), full(g1), full(g2), vec, vec],
        out_specs=(row,) * 7,
        scratch_shapes=[pltpu.VMEM((8, d), F32)],
        compiler_params=_cparams("arbitrary", "arbitrary"),
        name="rwkv_proj",
    )(x, shift, scale, ng, mu, wrkv, w0, w1, w2, a0, a1, a2, g1, g2, k_k, k_a)


def _rwkv_scan_kernel(r_ref, k_ref, v_ref, ld_ref, kk_ref, kb_ref, y_ref,
                      s_ref, ry_ref, y1_ref, rr_ref, q_ref, g_ref):
    c = SCAN_CHUNK
    n_chunks = r_ref.shape[0] // c
    n_groups = r_ref.shape[1] // LANES
    two_c = HEADS_PER_LANE_GROUP * c

    @pl.when(pl.program_id(2) == 0)
    def _():
        s_ref[...] = jnp.zeros_like(s_ref)

    ri = lax.broadcasted_iota(jnp.int32, (two_c, two_c), 0)
    ci = lax.broadcasted_iota(jnp.int32, (two_c, two_c), 1)
    same_head = (ri // c) == (ci // c)
    strict = jnp.logical_and(same_head, ri > ci)
    incl = jnp.logical_and(same_head, ri >= ci)
    eye = (ri == ci).astype(F32)
    ti = lax.broadcasted_iota(jnp.int32, (c, c), 0)
    tj = lax.broadcasted_iota(jnp.int32, (c, c), 1)
    tri3 = jnp.concatenate([(ti >= tj).astype(BF16)] * 3, axis=1)
    cumsum = lambda z: _dot(tri3, jnp.concatenate(_split_bf16(z, 3), axis=0))
    head0 = lax.broadcasted_iota(jnp.int32, (c, LANES), 1) < HEAD_DIM

    def stack(z):
        return jnp.concatenate([jnp.where(head0, z, 0.0), jnp.where(head0, 0.0, z)], axis=0)

    def prepare(chains, hooks=(lambda: None, lambda: None)):
        each = lambda f, *cols: [f(*args) for args in zip(*cols)]
        rows = [pl.ds(pl.multiple_of(ic * c, c), c) for ic, _ in chains]
        lanes = [slice(gi * LANES, (gi + 1) * LANES) for _, gi in chains]
        load = lambda ref: each(lambda rw, ln: ref[rw, ln].astype(F32), rows, lanes)
        ld = load(ld_ref)
        cum = each(cumsum, ld)
        tot = each(lambda z: z[c - 1:c, :], cum)
        kb, kx = load(kb_ref), load(k_ref)
        g_inv = each(lambda z: jnp.exp(-z), cum)
        g_rest = each(lambda t_, z: jnp.exp(t_ - z), tot, cum)
        at = each(lambda kn, z, l_: stack(-kn * jnp.exp(z - l_)), load(kk_ref), cum, ld)
        rt = each(lambda r_, z: stack(r_ * jnp.exp(z)), load(r_ref), cum)
        bh = each(lambda x_, g_: stack(x_ * g_), kb, g_inv)
        kh = each(lambda x_, g_: stack(x_ * g_), kx, g_inv)
        bc = each(lambda x_, g_: stack(x_ * g_), kb, g_rest)
        kc = each(lambda x_, g_: stack(x_ * g_), kx, g_rest)
        vs = each(stack, load(v_ref))
        bf = lambda zs: [z.astype(BF16) for z in zs]
        at_b, bc_b, vs_b = bf(at), bf(bc), bf(vs)
        hooks[0]()
        qa = each(lambda a_, r_: jnp.concatenate([a_, r_.astype(BF16)], axis=0), at_b, rt)
        bk = each(lambda b_, k_: jnp.concatenate([b_, k_], axis=0), bf(bh), bf(kh))
        mbk = each(_dot_nt, qa, bk)
        a_rb = bf(each(lambda z: jnp.where(incl, z[two_c:, :LANES], 0.0), mbk))
        a_ak = bf(each(lambda z: jnp.where(strict, z[:two_c, LANES:], 0.0), mbk))
        a_rk = bf(each(lambda z: jnp.where(incl, z[two_c:, LANES:], 0.0), mbk))
        av = bf(each(_dot, a_ak, vs_b))
        pw = each(lambda z: jnp.where(strict, z[:two_c, :LANES], 0.0), mbk)
        t_inv = each(lambda z: eye + z, pw)
        for _ in range(5):
            pw_b = bf(pw)
            pw = each(_dot, pw_b, pw_b)
            t_inv = each(lambda t_, p_: t_ + _dot(t_.astype(BF16), p_.astype(BF16)), t_inv, pw)
        hooks[1]()
        wu = bf(each(lambda t_, a_, v_: _dot(t_, jnp.concatenate([a_, v_], axis=1)), bf(t_inv), at_b, av))
        arb_wu = each(_dot, a_rb, wu)
        wu_bc = each(_dot_tn, wu, bc_b)
        ry = each(lambda r_, z: r_ + z[:, :LANES], rt, arb_wu)
        y1 = each(lambda z, b_, v_: z[:, LANES:] + _dot(b_, v_), arb_wu, a_rk, vs_b)
        rr = each(lambda z: z[:two_c], wu_bc)
        qq = each(lambda z, v_, k_: z[two_c:] + _dot_tn(v_, k_), wu_bc, vs_b, bf(kc))
        for j, (ic, gi) in enumerate(chains):
            ry_ref[gi, ic] = ry[j].astype(BF16)
            y1_ref[gi, ic] = y1[j][:c] + y1[j][c:]
            rr_ref[gi, ic] = rr[j].astype(BF16)
            q_ref[gi, ic] = qq[j]
            g_ref[gi, ic] = jnp.broadcast_to(jnp.exp(tot[j]), (8, LANES))

    def advance(ic, states):
        rows = pl.ds(pl.multiple_of(ic * c, c), c)
        groups = range(n_groups)
        s_b = [states[gi].astype(BF16) for gi in groups]
        sr = [_dot(s_b[gi], rr_ref[gi, ic]) for gi in groups]
        ys = [_dot_nt(ry_ref[gi, ic], s_b[gi]) for gi in groups]
        for gi in groups:
            y = ys[gi][:c] + ys[gi][c:] + y1_ref[gi, ic]
            y_ref[rows, gi * LANES:(gi + 1) * LANES] = y.astype(y_ref.dtype)
        return tuple(states[gi] * g_ref[gi, ic, 0:1, :] + sr[gi] + q_ref[gi, ic] for gi in groups)

    chains_of = lambda ig: [(ig * SCAN_UNROLL + u, gi) for u in range(SCAN_UNROLL)
                            for gi in range(n_groups)]
    n_iter = n_chunks // SCAN_UNROLL

    def pipelined(ig, states):
        holder = [states]

        def advance_hook(u):
            def run():
                holder[0] = advance((ig - 1) * SCAN_UNROLL + u, holder[0])
            return run

        prepare(chains_of(ig), hooks=(advance_hook(0), advance_hook(1)))
        return holder[0]

    prepare(chains_of(0))
    states = lax.fori_loop(1, n_iter, pipelined, tuple(s_ref[gi] for gi in range(n_groups)))
    for u in range(SCAN_UNROLL):
        states = advance((n_iter - 1) * SCAN_UNROLL + u, states)
    for gi in range(n_groups):
        s_ref[gi] = states[gi]


def _rwkv_scan(r, k, v, ld, kkn, kb):
    bsz, t, d = r.shape
    tb = SCAN_TIME_BLOCK
    n_groups = SCAN_GROUPS_PER_STEP
    n_chunks = tb // SCAN_CHUNK
    two_c = HEADS_PER_LANE_GROUP * SCAN_CHUNK
    blk = pl.BlockSpec((None, tb, n_groups * LANES), lambda b, hp, i: (b, i, hp))
    mats = lambda dt: pltpu.VMEM((n_groups, n_chunks, two_c, LANES), dt)
    return pl.pallas_call(
        _rwkv_scan_kernel,
        out_shape=jax.ShapeDtypeStruct((bsz, t, d), BF16),
        grid=(bsz, d // (n_groups * LANES), t // tb),
        in_specs=[blk] * 6,
        out_specs=blk,
        scratch_shapes=[pltpu.VMEM((n_groups, two_c, LANES), F32), mats(BF16),
                        pltpu.VMEM((n_groups, n_chunks, SCAN_CHUNK, LANES), F32), mats(BF16),
                        mats(F32),
                        pltpu.VMEM((n_groups, n_chunks, 8, LANES), F32)],
        compiler_params=_cparams("parallel", "parallel", "arbitrary"),
        name="rwkv_scan",
    )(r, k, v, ld, kkn, kb)


def _rwkv_gated(y_ref, r_ref, k_ref, v_ref, g_ref, lg_ref, lb_ref, rk_ref):
    f32 = lambda ref: ref[...].astype(F32)
    y = f32(y_ref)
    mean = _head_sum(y) * (1.0 / HEAD_DIM)
    dy = y - mean
    var = _head_sum(dy * dy) * (1.0 / HEAD_DIM)
    yn = dy * lax.rsqrt(var + GN_EPS) * lg_ref[...] + lb_ref[...]
    bonus = _head_sum(f32(r_ref) * f32(k_ref) * rk_ref[...]) * f32(v_ref)
    return ((yn + bonus) * f32(g_ref)).astype(BF16)


def _swiglu_residual(x, shift_ref, scale_ref, gate_ref, ng_ref, wg_ref, wu_ref, wd_ref):
    h = _ada_ln(x, ng_ref[...], shift_ref[...], scale_ref[...]).astype(BF16)
    gt = _dot(h, wg_ref[...])
    up = _dot(h, wu_ref[...])
    act = (gt * jax.nn.sigmoid(gt) * up).astype(BF16)
    return x + gate_ref[...] * _dot(act, wd_ref[...])


N_FFN_REFS = 7


def _mixer_ffn_kernel(*refs):
    *mixer_refs, x_ref, agate_ref, wo_ref = refs[:-N_FFN_REFS - 1]
    ffn_refs, o_ref = refs[-N_FFN_REFS - 1:-1], refs[-1]
    a = mixer_refs[0][...].astype(BF16) if len(mixer_refs) == 1 else _rwkv_gated(*mixer_refs)
    x = x_ref[...] + agate_ref[...] * _dot(a, wo_ref[...])
    o_ref[...] = _swiglu_residual(x, *ffn_refs)


def _mixer_ffn(rows, vecs, x, agate, wo, shift, scale, gate, ng, wg, wu, wd):
    bsz, t, d = x.shape
    f = wg.shape[1]
    tm = ROW_TILE
    row = pl.BlockSpec((None, tm, d), lambda b, i: (b, i, 0))
    mod = pl.BlockSpec((None, 1, d), lambda b, i: (b, 0, 0))
    vec = pl.BlockSpec((1, d), lambda b, i: (0, 0))
    const = lambda shape: pl.BlockSpec(shape, lambda b, i: (0, 0), pipeline_mode=pl.Buffered(1))
    specs = ([row] * len(rows) + [vec] * len(vecs) + [row, mod, const((d, d))]
             + [mod, mod, mod, vec, const((d, f)), const((d, f)), const((f, d))])
    assert len(specs) == len(rows) + len(vecs) + 3 + N_FFN_REFS
    return pl.pallas_call(
        _mixer_ffn_kernel,
        out_shape=jax.ShapeDtypeStruct((bsz, t, d), F32),
        grid=(bsz, t // tm),
        in_specs=specs,
        out_specs=row,
        compiler_params=_cparams("parallel", "parallel"),
        name="mixer_ffn",
    )(*rows, *vecs, x, agate, wo, shift, scale, gate, ng, wg, wu, wd)


def _head_rms(z, g):
    ms = _head_sum(z * z) * (1.0 / HEAD_DIM)
    return z * lax.rsqrt(ms + NORM_EPS) * g


def _moba_qkv_kernel(x_ref, qshift_ref, qscale_ref, qng_ref, kshift_ref, kscale_ref, kng_ref,
                     wq_ref, wk_ref, wv_ref, qg_ref, kg_ref, qt_out, k_out, vt_out, km_out):
    x = x_ref[...]
    xn = x * lax.rsqrt(jnp.mean(x * x, axis=-1, keepdims=True) + NORM_EPS)
    hq = ((xn * qng_ref[...]) * (1.0 + qscale_ref[...]) + qshift_ref[...]).astype(BF16)
    hk = ((xn * kng_ref[...]) * (1.0 + kscale_ref[...]) + kshift_ref[...]).astype(BF16)
    qt_out[...] = _head_rms(_dot(hq, wq_ref[...]), qg_ref[...]).T
    k = _head_rms(_dot(hk, wk_ref[...]), kg_ref[...])
    k_out[...] = k.astype(BF16)
    vt_out[...] = _dot(hk, wv_ref[...]).T.astype(BF16)
    km_out[...] = jnp.mean(k, axis=0, keepdims=True)


def _moba_qkv(x, qshift, qscale, qng, kshift, kscale, kng, wq, wk, wv, qg, kg):
    bsz, t, d = x.shape
    tm = MOBA_BLOCK
    nb = t // tm
    row = pl.BlockSpec((None, tm, d), lambda b, i: (b, i, 0))
    mod = pl.BlockSpec((None, 1, d), lambda b, i: (b, 0, 0))
    vec = pl.BlockSpec((1, d), lambda b, i: (0, 0))
    mat = pl.BlockSpec((d, d), lambda b, i: (0, 0))
    return pl.pallas_call(
        _moba_qkv_kernel,
        out_shape=(jax.ShapeDtypeStruct((bsz, d, t), F32),
                   jax.ShapeDtypeStruct((bsz, t, d), BF16),
                   jax.ShapeDtypeStruct((bsz, nb, d, tm), BF16),
                   jax.ShapeDtypeStruct((bsz * nb, 1, d), F32)),
        grid=(bsz, nb),
        in_specs=[row, mod, mod, vec, mod, mod, vec, mat, mat, mat, vec, vec],
        out_specs=(pl.BlockSpec((None, d, tm), lambda b, i: (b, 0, i)), row,
                   pl.BlockSpec((None, None, d, tm), lambda b, i: (b, i, 0, 0)),
                   pl.BlockSpec((None, 1, d), lambda b, i: (b * nb + i, 0, 0))),
        compiler_params=_cparams("parallel", "parallel"),
        name="moba_qkv",
    )(x, qshift, qscale, qng, kshift, kscale, kng, wq, wk, wv, qg, kg)


def _moba_attn_kernel(qt_ref, k_ref, vt_ref, bias_ref, o_ref, qs_ref, sa_ref, sb_ref, p_ref):
    blk = MOBA_BLOCK
    nb = bias_ref.shape[1]
    qb = pl.program_id(2)
    heads = range(qs_ref.shape[0])
    group = lambda hd: slice(hd // HEADS_PER_LANE_GROUP * LANES, (hd // HEADS_PER_LANE_GROUP + 1) * LANES)
    neg_inf = F32(-jnp.inf)
    chan_head = lax.broadcasted_iota(jnp.int32, (LANES, blk), 0) // HEAD_DIM
    key_i = lax.broadcasted_iota(jnp.int32, (blk, blk), 0)
    qry_i = lax.broadcasted_iota(jnp.int32, (blk, blk), 1)
    causal = key_i <= qry_i
    q_scale = F32(HEAD_DIM ** -0.5 * LOG2_E)
    for hd in heads:
        keep = chan_head == hd % HEADS_PER_LANE_GROUP
        qs_ref[hd] = (jnp.where(keep, qt_ref[group(hd), :], 0.0) * q_scale).astype(BF16)

    def put_scores(n, dst_ref):
        rows = pl.ds(pl.multiple_of(n * blk, blk), blk)
        for hd in heads:
            dst_ref[hd] = _dot(k_ref[rows, group(hd)], qs_ref[hd])

    def bias_rows(n):
        return [bias_ref[hd, pl.ds(n, 1), :] for hd in heads]

    ones_rows = jnp.ones((SUM_ROWS, blk), BF16)

    def weighted_values(n, p, rows):
        vtn = vt_ref[n]
        lhs = [jnp.concatenate([vtn[hd * HEAD_DIM:(hd + 1) * HEAD_DIM, :], ones_rows], axis=0)
               for hd in heads]
        return [jnp.where(rows[hd] == 0.0, _dot(lhs[hd], p[hd]), 0.0) for hd in heads]

    def softmax_step(src_ref, m, rows):
        s = [src_ref[hd] for hd in heads]
        m_new = [jnp.maximum(m[hd], jnp.max(s[hd], axis=0, keepdims=True) + rows[hd]) for hd in heads]
        alpha = [jnp.exp2(m[hd] - m_new[hd]) for hd in heads]
        p = [jnp.exp2(s[hd] - m_new[hd]).astype(BF16) for hd in heads]
        return m_new, alpha, p

    past_of = lambda t: jnp.clip(t - 1, 0, nb - 1)
    values_of = lambda t: jnp.where(t == 0, qb, past_of(t))
    rows_of = lambda t: [jnp.where(t == 0, 0.0, r) for r in bias_rows(past_of(t))]

    def tile_pair(j, carry):
        m, acc, alpha_prev = carry
        a, b = 2 * j, 2 * j + 1
        prev = jnp.maximum(a - 1, 0)
        rows_a, rows_b = rows_of(a), rows_of(b)
        put_scores(past_of(b), sb_ref)
        pv_prev = weighted_values(values_of(prev), [p_ref[hd] for hd in heads], rows_of(prev))
        m, alpha_a, p_a = softmax_step(sa_ref, m, rows_a)
        put_scores(past_of(b + 1), sa_ref)
        pv_a = weighted_values(values_of(a), p_a, rows_a)
        m, alpha_b, p_b = softmax_step(sb_ref, m, rows_b)
        for hd in heads:
            p_ref[hd] = p_b[hd]
        acc = [alpha_a[hd] * (alpha_prev[hd] * acc[hd] + pv_prev[hd]) + pv_a[hd] for hd in heads]
        return m, acc, alpha_b

    own = pl.ds(pl.multiple_of(qb * blk, blk), blk)
    for hd in heads:
        sa_ref[hd] = jnp.where(causal, _dot(k_ref[own, group(hd)], qs_ref[hd]), neg_inf)
    p_ref[...] = jnp.zeros_like(p_ref)
    init = ([jnp.full((1, blk), neg_inf, F32) for _ in heads],
            [jnp.zeros((HEAD_DIM + SUM_ROWS, blk), F32) for _ in heads],
            [jnp.ones((1, blk), F32) for _ in heads])
    n_pairs = (qb + 2) // 2
    m, acc, alpha_prev = lax.fori_loop(0, n_pairs, tile_pair, init)
    last = 2 * n_pairs - 1
    pv_last = weighted_values(values_of(last), [p_ref[hd] for hd in heads], rows_of(last))
    acc = [alpha_prev[hd] * acc[hd] + pv_last[hd] for hd in heads]
    ot = jnp.concatenate([acc[hd][:HEAD_DIM] / acc[hd][HEAD_DIM:HEAD_DIM + 1] for hd in heads], axis=0)
    o_ref[...] = ot.T


def _moba_attn(qt, k, vt, bias):
    bsz, d, t = qt.shape
    nb = t // MOBA_BLOCK
    width = ATTN_GROUPS_PER_STEP * LANES
    hpg = ATTN_GROUPS_PER_STEP * HEADS_PER_LANE_GROUP
    return pl.pallas_call(
        _moba_attn_kernel,
        out_shape=jax.ShapeDtypeStruct((bsz, t, d), F32),
        grid=(bsz, d // width, nb),
        in_specs=[pl.BlockSpec((None, width, MOBA_BLOCK), lambda b, hp, i: (b, hp, i)),
                  pl.BlockSpec((None, t, width), lambda b, hp, i: (b, 0, hp)),
                  pl.BlockSpec((None, nb, width, MOBA_BLOCK), lambda b, hp, i: (b, 0, hp, 0)),
                  pl.BlockSpec((None, None, hpg, nb, MOBA_BLOCK), lambda b, hp, i: (b, i, hp, 0, 0))],
        out_specs=pl.BlockSpec((None, MOBA_BLOCK, width), lambda b, hp, i: (b, i, hp)),
        scratch_shapes=[pltpu.VMEM((hpg, LANES, MOBA_BLOCK), BF16),
                        pltpu.VMEM((hpg, MOBA_BLOCK, MOBA_BLOCK), F32),
                        pltpu.VMEM((hpg, MOBA_BLOCK, MOBA_BLOCK), F32),
                        pltpu.VMEM((hpg, MOBA_BLOCK, MOBA_BLOCK), BF16)],
        compiler_params=_cparams("parallel", "parallel", "arbitrary"),
        name="moba_attn",
    )(qt, k, vt, bias)


def _moba_select_kernel(qt_ref, km_ref, bias_ref):
    nb = km_ref.shape[0]
    blk = qt_ref.shape[1]
    qb = pl.program_id(1)
    neg_inf = F32(-jnp.inf)
    rows = HEADS_PER_LANE_GROUP * nb
    row_head = lax.broadcasted_iota(jnp.int32, (rows, LANES), 0) // nb
    chan_head = lax.broadcasted_iota(jnp.int32, (rows, LANES), 1) // HEAD_DIM
    gates = []
    for g in range(qt_ref.shape[0] // LANES):
        km = km_ref[:, g * LANES:(g + 1) * LANES]
        km2 = jnp.where(row_head == chan_head, jnp.concatenate([km] * HEADS_PER_LANE_GROUP, axis=0), 0.0)
        gates.append(jnp.dot(km2, qt_ref[g * LANES:(g + 1) * LANES, :], precision=HIGHEST,
                             preferred_element_type=F32))
    gate = jnp.concatenate(gates, axis=0).reshape(N_HEADS, nb, blk)
    blk_iota = lax.broadcasted_iota(jnp.int32, gate.shape, 1)
    valid = blk_iota < qb
    gate = jnp.where(valid, gate, neg_inf)
    sel = jnp.zeros(gate.shape, jnp.bool_)
    for _ in range(MOBA_TOPK):
        mx = jnp.max(gate, axis=1, keepdims=True)
        idx = jnp.min(jnp.where(gate == mx, blk_iota, nb), axis=1, keepdims=True)
        hit = blk_iota == idx
        sel = jnp.logical_or(sel, hit)
        gate = jnp.where(hit, neg_inf, gate)
    bias_ref[...] = jnp.where(jnp.logical_and(sel, valid), 0.0, neg_inf)


def _moba_select(qt, km):
    bsz, d, t = qt.shape
    nb = km.shape[1]
    return pl.pallas_call(
        _moba_select_kernel,
        out_shape=jax.ShapeDtypeStruct((bsz, nb, N_HEADS, nb, MOBA_BLOCK), F32),
        grid=(bsz, nb),
        in_specs=[pl.BlockSpec((None, d, MOBA_BLOCK), lambda b, i: (b, 0, i)),
                  pl.BlockSpec((None, nb, d), lambda b, i: (b, 0, 0))],
        out_specs=pl.BlockSpec((None, None, N_HEADS, nb, MOBA_BLOCK), lambda b, i: (b, i, 0, 0, 0)),
        compiler_params=_cparams("parallel", "parallel"),
        name="moba_select",
    )(qt, km)


def _pad_rank(w_down, w_up):
    rank = w_down.shape[1]
    pad = -rank % LANES
    return (jnp.pad(w_down, ((0, 0), (0, pad))).astype(BF16),
            jnp.pad(w_up, ((0, pad), (0, 0))).astype(BF16))


def kernel(x, c, norm_g, w_ada, b_ada, rw_mu, rw_w_rkv, rw_w0, rw_w1, rw_w2, rw_a0, rw_a1, rw_a2, rw_g1, rw_g2, rw_k_k, rw_k_a, rw_r_k, rw_lnx_g, rw_lnx_b, rw_w_o, ffn_w_gate, ffn_w_up, ffn_w_down, kv_norm_g, kv_w_ada, kv_b_ada, kv_w_k, kv_w_v, k_norm_g, mb_w_q, mb_q_norm_g, mb_w_o):
    depth = norm_g.shape[0]
    d = x.shape[-1]
    assert depth == 2 and rw_mu.shape[0] == 1 and mb_w_q.shape[0] == 1 and d == D_MODEL

    mods = _ada(c, w_ada.reshape(2 * depth, d, 3 * d), b_ada.reshape(2 * depth, 1, 3 * d))
    kv_mods = _ada(c, kv_w_ada[None], kv_b_ada[None, None])[0]
    part = lambda m, j: m[:, None, j * d:(j + 1) * d]
    vec = lambda a: a.reshape(1, d)

    m = mods[0]
    w1, w2 = _pad_rank(rw_w1[0], rw_w2[0])
    a1, a2 = _pad_rank(rw_a1[0], rw_a2[0])
    g1, g2 = _pad_rank(rw_g1[0], rw_g2[0])
    r, k, v, ld, kkn, kb, g = _rwkv_proj(
        x, part(m, 0), part(m, 1), vec(norm_g[0, 0]), rw_mu[0], rw_w_rkv[0].astype(BF16),
        vec(rw_w0[0]), w1, w2, vec(rw_a0[0]), a1, a2, g1, g2, vec(rw_k_k[0]), vec(rw_k_a[0]))
    y = _rwkv_scan(r, k, v, ld, kkn, kb)
    f = mods[1]
    x = _mixer_ffn((y, r, k, v, g), (vec(rw_lnx_g[0]), vec(rw_lnx_b[0]), vec(rw_r_k[0])),
                   x, part(m, 2), rw_w_o[0].astype(BF16),
                   part(f, 0), part(f, 1), part(f, 2), vec(norm_g[0, 1]),
                   ffn_w_gate[0].astype(BF16), ffn_w_up[0].astype(BF16), ffn_w_down[0].astype(BF16))

    m = mods[2]
    q, kx, vx, km = _moba_qkv(
        x, part(m, 0), part(m, 1), vec(norm_g[1, 0]), part(kv_mods, 0), part(kv_mods, 1),
        vec(kv_norm_g), mb_w_q[0].astype(BF16), kv_w_k.astype(BF16), kv_w_v.astype(BF16),
        vec(jnp.tile(mb_q_norm_g[0], N_HEADS)), vec(jnp.tile(k_norm_g, N_HEADS)))
    o = _moba_attn(q, kx, vx, _moba_select(q, km.reshape(x.shape[0], -1, d)))
    f = mods[3]
    return _mixer_ffn((o,), (), x, part(m, 2), mb_w_o[0].astype(BF16),
                      part(f, 0), part(f, 1), part(f, 2), vec(norm_g[1, 1]),
                      ffn_w_gate[1].astype(BF16), ffn_w_up[1].astype(BF16), ffn_w_down[1].astype(BF16))
```

```python
import jax
import jax.numpy as jnp
from jax import lax
from jax.experimental import pallas as pl
from jax.experimental.pallas import tpu as pltpu

F32 = jnp.float32
BF16 = jnp.bfloat16
HIGHEST = lax.Precision.HIGHEST

D_MODEL = 1024
HEAD_DIM = 64
N_HEADS = D_MODEL // HEAD_DIM
MOBA_BLOCK = 256
MOBA_TOPK = 3
NORM_EPS = 1e-6
GN_EPS = 64e-5
L2_EPS = 1e-12
LOG2_E = 1.4426950408889634

LANES = 128
MXU_WIDTH = 256
HEADS_PER_LANE_GROUP = LANES // HEAD_DIM
VMEM_LIMIT_BYTES = 56 * 1024 * 1024
SCAN_CHUNK = 64
ROW_TILE = 256
FFN_ROW_TILE = 256
SCAN_TIME_BLOCK = 512
SCAN_GROUPS_PER_STEP = 4
ATTN_GROUPS_PER_STEP = 1
SUM_ROWS = 16
SCAN_UNROLL = 2


def _cparams(*sem):
    return pltpu.CompilerParams(dimension_semantics=sem, vmem_limit_bytes=VMEM_LIMIT_BYTES)


def _dot(a, b):
    return jnp.dot(a, b, preferred_element_type=F32)


def _dot_nt(a, b, precision=None):
    return lax.dot_general(a, b, (((1,), (1,)), ((), ())), precision=precision,
                           preferred_element_type=F32)


def _dot_tn(a, b):
    return lax.dot_general(a, b, (((0,), (0,)), ((), ())), preferred_element_type=F32)


def _head_ones(width):
    r = lax.broadcasted_iota(jnp.int32, (width, width), 0) // HEAD_DIM
    c = lax.broadcasted_iota(jnp.int32, (width, width), 1) // HEAD_DIM
    return (r == c).astype(BF16)


def _split_bf16(x, terms):
    parts = []
    for _ in range(terms - 1):
        hi = x.astype(BF16)
        parts.append(hi)
        x = x - hi.astype(F32)
    return parts + [x.astype(BF16)]


def _head_sum(x):
    xb = x.astype(BF16)
    ones = _head_ones(MXU_WIDTH)
    parts = [_dot(xb[:, j * MXU_WIDTH:(j + 1) * MXU_WIDTH], ones)
             for j in range(x.shape[1] // MXU_WIDTH)]
    return jnp.concatenate(parts, axis=1)


def _ada_ln(x, g, shift, scale):
    inv = lax.rsqrt(jnp.mean(x * x, axis=-1, keepdims=True) + NORM_EPS)
    return (x * inv * g) * (1.0 + scale) + shift


def _ada_kernel(c_ref, w_ref, b_ref, o_ref):
    c = c_ref[...]
    s = c * jax.nn.sigmoid(c)
    o_ref[...] = jnp.dot(s, w_ref[...], precision=HIGHEST, preferred_element_type=F32) + b_ref[...]


def _ada(c, w, b):
    n, d, md = w.shape
    bsz = c.shape[0]
    return pl.pallas_call(
        _ada_kernel,
        out_shape=jax.ShapeDtypeStruct((n, bsz, md), F32),
        grid=(n, md // d),
        in_specs=[pl.BlockSpec((bsz, d), lambda i, j: (0, 0)),
                  pl.BlockSpec((None, d, d), lambda i, j: (i, 0, j)),
                  pl.BlockSpec((None, 1, d), lambda i, j: (i, 0, j))],
        out_specs=pl.BlockSpec((None, bsz, d), lambda i, j: (i, 0, j)),
        compiler_params=_cparams("parallel", "parallel"),
        name="ada",
    )(c, w, b)


def _rwkv_proj_kernel(x_ref, shift_ref, scale_ref, ng_ref, mu_ref, wrkv_ref, w0_ref, w1_ref,
                      w2_ref, a0_ref, a1_ref, a2_ref, g1_ref, g2_ref, kk_ref, ka_ref,
                      r_out, k_out, v_out, ld_out, kkn_out, kb_out, g_out, carry_ref):
    tm = x_ref.shape[0]
    h = _ada_ln(x_ref[...], ng_ref[...], shift_ref[...], scale_ref[...])
    first = pl.program_id(1) == 0
    prev_last = jnp.where(first, 0.0, carry_ref[7:8, :])
    row = lax.broadcasted_iota(jnp.int32, h.shape, 0)
    h_prev = jnp.where(row == 0, prev_last, pltpu.roll(h, 1, axis=0))
    carry_ref[...] = h[tm - 8:tm, :]
    d = h_prev - h
    mix = lambda i: (h + d * mu_ref[i:i + 1, :]).astype(BF16)

    r = _dot(mix(0), wrkv_ref[0])
    k = _dot(mix(1), wrkv_ref[1])
    v = _dot(mix(2), wrkv_ref[2])
    zw = w0_ref[...] + _dot(jnp.tanh(_dot(mix(3), w1_ref[...])).astype(BF16), w2_ref[...])
    a = jax.nn.sigmoid(a0_ref[...] + _dot(_dot(mix(4), a1_ref[...]).astype(BF16), a2_ref[...]))
    g = _dot(jax.nn.sigmoid(_dot(mix(5), g1_ref[...])).astype(BF16), g2_ref[...])

    ld_out[...] = -jnp.exp(F32(-0.5)) * jax.nn.sigmoid(zw)
    kk = k * kk_ref[...]
    nrm = jnp.maximum(jnp.sqrt(_head_sum(kk * kk)), L2_EPS)
    kk = kk / nrm
    r_out[...] = r.astype(r_out.dtype)
    k_out[...] = (k * (1.0 + (a - 1.0) * ka_ref[...])).astype(k_out.dtype)
    v_out[...] = v.astype(v_out.dtype)
    kkn_out[...] = kk.astype(kkn_out.dtype)
    kb_out[...] = (kk * a).astype(kb_out.dtype)
    g_out[...] = g.astype(g_out.dtype)


def _rwkv_proj(x, shift, scale, ng, mu, wrkv, w0, w1, w2, a0, a1, a2, g1, g2, k_k, k_a):
    bsz, t, d = x.shape
    tm = ROW_TILE
    row = pl.BlockSpec((None, tm, d), lambda b, i: (b, i, 0))
    mod = pl.BlockSpec((None, 1, d), lambda b, i: (b, 0, 0))
    vec = pl.BlockSpec((1, d), lambda b, i: (0, 0))
    full = lambda a: pl.BlockSpec(a.shape, lambda b, i: (0,) * a.ndim)
    out = lambda dt: jax.ShapeDtypeStruct((bsz, t, d), dt)
    return pl.pallas_call(
        _rwkv_proj_kernel,
        out_shape=(out(BF16), out(BF16), out(BF16), out(F32), out(BF16), out(BF16), out(BF16)),
        grid=(bsz, t // tm),
        in_specs=[row, mod, mod, vec, full(mu), full(wrkv), vec, full(w1), full(w2), vec,
                  full(a1), full(a2), full(g1), full(g2), vec, vec],
        out_specs=(row,) * 7,
        scratch_shapes=[pltpu.VMEM((8, d), F32)],
        compiler_params=_cparams("arbitrary", "arbitrary"),
        name="rwkv_proj",
    )(x, shift, scale, ng, mu, wrkv, w0, w1, w2, a0, a1, a2, g1, g2, k_k, k_a)


def _rwkv_scan_kernel(r_ref, k_ref, v_ref, ld_ref, kk_ref, kb_ref, y_ref,
                      s_ref, ry_ref, y1_ref, rr_ref, q_ref, g_ref):
    c = SCAN_CHUNK
    n_chunks = r_ref.shape[0] // c
    n_groups = r_ref.shape[1] // LANES
    two_c = HEADS_PER_LANE_GROUP * c

    @pl.when(pl.program_id(2) == 0)
    def _():
        s_ref[...] = jnp.zeros_like(s_ref)

    ri = lax.broadcasted_iota(jnp.int32, (two_c, two_c), 0)
    ci = lax.broadcasted_iota(jnp.int32, (two_c, two_c), 1)
    same_head = (ri // c) == (ci // c)
    strict = jnp.logical_and(same_head, ri > ci)
    incl = jnp.logical_and(same_head, ri >= ci)
    both_tri = jnp.concatenate([strict, incl], axis=0)
    eye = (ri == ci).astype(F32)
    ti = lax.broadcasted_iota(jnp.int32, (c, c), 0)
    tj = lax.broadcasted_iota(jnp.int32, (c, c), 1)
    tri3 = jnp.concatenate([(ti >= tj).astype(BF16)] * 3, axis=1)
    cumsum = lambda z: _dot(tri3, jnp.concatenate(_split_bf16(z, 3), axis=0))
    head0 = lax.broadcasted_iota(jnp.int32, (c, LANES), 1) < HEAD_DIM

    def stack(z):
        return jnp.concatenate([jnp.where(head0, z, 0.0), jnp.where(head0, 0.0, z)], axis=0)

    def prepare(chains, hooks=(lambda: None, lambda: None)):
        each = lambda f, *cols: [f(*args) for args in zip(*cols)]
        rows = [pl.ds(pl.multiple_of(ic * c, c), c) for ic, _ in chains]
        lanes = [slice(gi * LANES, (gi + 1) * LANES) for _, gi in chains]
        load = lambda ref: each(lambda rw, ln: ref[rw, ln].astype(F32), rows, lanes)
        ld = load(ld_ref)
        cum = each(cumsum, ld)
        tot = each(lambda z: z[c - 1:c, :], cum)
        kb, kx = load(kb_ref), load(k_ref)
        g_inv = each(lambda z: jnp.exp(-z), cum)
        g_rest = each(lambda t_, z: jnp.exp(t_ - z), tot, cum)
        at = each(lambda kn, z, l_: stack(-kn * jnp.exp(z - l_)), load(kk_ref), cum, ld)
        rt = each(lambda r_, z: stack(r_ * jnp.exp(z)), load(r_ref), cum)
        bh = each(lambda x_, g_: stack(x_ * g_), kb, g_inv)
        kh = each(lambda x_, g_: stack(x_ * g_), kx, g_inv)
        bc = each(lambda x_, g_: stack(x_ * g_), kb, g_rest)
        kc = each(lambda x_, g_: stack(x_ * g_), kx, g_rest)
        vs = each(stack, load(v_ref))
        bf = lambda zs: [z.astype(BF16) for z in zs]
        at_b, bc_b, vs_b = bf(at), bf(bc), bf(vs)
        hooks[0]()
        qa = each(lambda a_, r_: jnp.concatenate([a_, r_.astype(BF16)], axis=0), at_b, rt)
        bk = each(lambda b_, k_: jnp.concatenate([b_, k_], axis=0), bf(bh), bf(kh))
        mbk = each(_dot_nt, qa, bk)
        a_rb = bf(each(lambda z: jnp.where(incl, z[two_c:, :LANES], 0.0), mbk))
        akk = bf(each(lambda z: jnp.where(both_tri, z[:, LANES:], 0.0), mbk))
        akk_v = each(_dot, akk, vs_b)
        av = bf(each(lambda z: z[:two_c], akk_v))
        pw = each(lambda z: jnp.where(strict, z[:two_c, :LANES], 0.0), mbk)
        t_inv = each(lambda z: eye + z, pw)
        pw_b = bf(pw)
        pw_b = bf(each(_dot, pw_b, pw_b))
        for _ in range(4):
            both = each(lambda p_, t_: _dot(jnp.concatenate([p_, t_.astype(BF16)], axis=0), p_),
                        pw_b, t_inv)
            t_inv = each(lambda t_, z: t_ + z[two_c:], t_inv, both)
            pw_b = bf(each(lambda z: z[:two_c], both))
        t_inv = each(lambda t_, p_: t_ + _dot(t_.astype(BF16), p_), t_inv, pw_b)
        hooks[1]()
        wu = bf(each(lambda t_, a_, v_: _dot(t_, jnp.concatenate([a_, v_], axis=1)), bf(t_inv), at_b, av))
        arb_wu = each(_dot, a_rb, wu)
        wu_bc = each(_dot_tn, wu, bc_b)
        ry = each(lambda r_, z: r_ + z[:, :LANES], rt, arb_wu)
        y1 = each(lambda z, w_: z[:, LANES:] + w_[two_c:], arb_wu, akk_v)
        rr = each(lambda z: z[:two_c], wu_bc)
        qq = each(lambda z, v_, k_: z[two_c:] + _dot_tn(v_, k_), wu_bc, vs_b, bf(kc))
        for j, (ic, gi) in enumerate(chains):
            ry_ref[gi, ic] = ry[j].astype(BF16)
            y1_ref[gi, ic] = y1[j][:c] + y1[j][c:]
            rr_ref[gi, ic] = rr[j].astype(BF16)
            q_ref[gi, ic] = qq[j]
            g_ref[gi, ic] = jnp.broadcast_to(jnp.exp(tot[j]), (8, LANES))

    def advance(ic, states):
        rows = pl.ds(pl.multiple_of(ic * c, c), c)
        groups = range(n_groups)
        s_b = [states[gi].astype(BF16) for gi in groups]
        sr = [_dot(s_b[gi], rr_ref[gi, ic]) for gi in groups]
        ys = [_dot_nt(ry_ref[gi, ic], s_b[gi]) for gi in groups]
        for gi in groups:
            y = ys[gi][:c] + ys[gi][c:] + y1_ref[gi, ic]
            y_ref[rows, gi * LANES:(gi + 1) * LANES] = y.astype(y_ref.dtype)
        return tuple(states[gi] * g_ref[gi, ic, 0:1, :] + sr[gi] + q_ref[gi, ic] for gi in groups)

    chains_of = lambda ig: [(ig * SCAN_UNROLL + u, gi) for u in range(SCAN_UNROLL)
                            for gi in range(n_groups)]
    n_iter = n_chunks // SCAN_UNROLL

    def pipelined(ig, states):
        holder = [states]

        def advance_hook(u):
            def run():
                holder[0] = advance((ig - 1) * SCAN_UNROLL + u, holder[0])
            return run

        prepare(chains_of(ig), hooks=(advance_hook(0), advance_hook(1)))
        return holder[0]

    prepare(chains_of(0))
    states = lax.fori_loop(1, n_iter, pipelined, tuple(s_ref[gi] for gi in range(n_groups)))
    for u in range(SCAN_UNROLL):
        states = advance((n_iter - 1) * SCAN_UNROLL + u, states)
    for gi in range(n_groups):
        s_ref[gi] = states[gi]


def _rwkv_scan(r, k, v, ld, kkn, kb):
    bsz, t, d = r.shape
    tb = SCAN_TIME_BLOCK
    n_groups = SCAN_GROUPS_PER_STEP
    n_chunks = tb // SCAN_CHUNK
    two_c = HEADS_PER_LANE_GROUP * SCAN_CHUNK
    blk = pl.BlockSpec((None, tb, n_groups * LANES), lambda b, hp, i: (b, i, hp))
    mats = lambda dt: pltpu.VMEM((n_groups, n_chunks, two_c, LANES), dt)
    return pl.pallas_call(
        _rwkv_scan_kernel,
        out_shape=jax.ShapeDtypeStruct((bsz, t, d), BF16),
        grid=(bsz, d // (n_groups * LANES), t // tb),
        in_specs=[blk] * 6,
        out_specs=blk,
        scratch_shapes=[pltpu.VMEM((n_groups, two_c, LANES), F32), mats(BF16),
                        pltpu.VMEM((n_groups, n_chunks, SCAN_CHUNK, LANES), F32), mats(BF16),
                        mats(F32),
                        pltpu.VMEM((n_groups, n_chunks, 8, LANES), F32)],
        compiler_params=_cparams("parallel", "parallel", "arbitrary"),
        name="rwkv_scan",
    )(r, k, v, ld, kkn, kb)


def _rwkv_gated(y_ref, r_ref, k_ref, v_ref, g_ref, lg_ref, lb_ref, rk_ref):
    f32 = lambda ref: ref[...].astype(F32)
    y = f32(y_ref)
    mean = _head_sum(y) * (1.0 / HEAD_DIM)
    dy = y - mean
    var = _head_sum(dy * dy) * (1.0 / HEAD_DIM)
    yn = dy * lax.rsqrt(var + GN_EPS) * lg_ref[...] + lb_ref[...]
    bonus = _head_sum(f32(r_ref) * f32(k_ref) * rk_ref[...]) * f32(v_ref)
    return ((yn + bonus) * f32(g_ref)).astype(BF16)


def _swiglu_residual(x, shift_ref, scale_ref, gate_ref, ng_ref, wg_ref, wu_ref, wd_ref):
    h = _ada_ln(x, ng_ref[...], shift_ref[...], scale_ref[...]).astype(BF16)
    gt = _dot(h, wg_ref[...])
    up = _dot(h, wu_ref[...])
    act = (gt * jax.nn.sigmoid(gt) * up).astype(BF16)
    return x + gate_ref[...] * _dot(act, wd_ref[...])


N_FFN_REFS = 7


def _mixer_ffn_kernel(*refs):
    *mixer_refs, x_ref, agate_ref, wo_ref = refs[:-N_FFN_REFS - 1]
    ffn_refs, o_ref = refs[-N_FFN_REFS - 1:-1], refs[-1]
    a = mixer_refs[0][...].astype(BF16) if len(mixer_refs) == 1 else _rwkv_gated(*mixer_refs)
    x = x_ref[...] + agate_ref[...] * _dot(a, wo_ref[...])
    o_ref[...] = _swiglu_residual(x, *ffn_refs)


def _mixer_ffn(rows, vecs, x, agate, wo, shift, scale, gate, ng, wg, wu, wd):
    bsz, t, d = x.shape
    f = wg.shape[1]
    tm = FFN_ROW_TILE
    row = pl.BlockSpec((None, tm, d), lambda b, i: (b, i, 0))
    mod = pl.BlockSpec((None, 1, d), lambda b, i: (b, 0, 0))
    vec = pl.BlockSpec((1, d), lambda b, i: (0, 0))
    const = lambda shape: pl.BlockSpec(shape, lambda b, i: (0, 0), pipeline_mode=pl.Buffered(1))
    specs = ([row] * len(rows) + [vec] * len(vecs) + [row, mod, const((d, d))]
             + [mod, mod, mod, vec, const((d, f)), const((d, f)), const((f, d))])
    assert len(specs) == len(rows) + len(vecs) + 3 + N_FFN_REFS
    return pl.pallas_call(
        _mixer_ffn_kernel,
        out_shape=jax.ShapeDtypeStruct((bsz, t, d), F32),
        grid=(bsz, t // tm),
        in_specs=specs,
        out_specs=row,
        compiler_params=_cparams("parallel", "parallel"),
        name="mixer_ffn",
    )(*rows, *vecs, x, agate, wo, shift, scale, gate, ng, wg, wu, wd)


def _head_rms(z, g):
    ms = _head_sum(z * z) * (1.0 / HEAD_DIM)
    return z * lax.rsqrt(ms + NORM_EPS) * g


def _moba_qkv_kernel(x_ref, qshift_ref, qscale_ref, qng_ref, kshift_ref, kscale_ref, kng_ref,
                     wq_ref, wk_ref, wv_ref, qg_ref, kg_ref, qt_out, k_out, vt_out, km_out):
    x = x_ref[...]
    xn = x * lax.rsqrt(jnp.mean(x * x, axis=-1, keepdims=True) + NORM_EPS)
    hq = ((xn * qng_ref[...]) * (1.0 + qscale_ref[...]) + qshift_ref[...]).astype(BF16)
    hk = ((xn * kng_ref[...]) * (1.0 + kscale_ref[...]) + kshift_ref[...]).astype(BF16)
    qt_out[...] = _head_rms(_dot(hq, wq_ref[...]), qg_ref[...]).T
    k = _head_rms(_dot(hk, wk_ref[...]), kg_ref[...])
    k_out[...] = k.astype(BF16)
    vt_out[...] = _dot(hk, wv_ref[...]).T.astype(BF16)
    km_out[...] = jnp.mean(k, axis=0, keepdims=True)


def _moba_qkv(x, qshift, qscale, qng, kshift, kscale, kng, wq, wk, wv, qg, kg):
    bsz, t, d = x.shape
    tm = MOBA_BLOCK
    nb = t // tm
    row = pl.BlockSpec((None, tm, d), lambda b, i: (b, i, 0))
    mod = pl.BlockSpec((None, 1, d), lambda b, i: (b, 0, 0))
    vec = pl.BlockSpec((1, d), lambda b, i: (0, 0))
    mat = pl.BlockSpec((d, d), lambda b, i: (0, 0))
    return pl.pallas_call(
        _moba_qkv_kernel,
        out_shape=(jax.ShapeDtypeStruct((bsz, d, t), F32),
                   jax.ShapeDtypeStruct((bsz, t, d), BF16),
                   jax.ShapeDtypeStruct((bsz, nb, d, tm), BF16),
                   jax.ShapeDtypeStruct((bsz * nb, 1, d), F32)),
        grid=(bsz, nb),
        in_specs=[row, mod, mod, vec, mod, mod, vec, mat, mat, mat, vec, vec],
        out_specs=(pl.BlockSpec((None, d, tm), lambda b, i: (b, 0, i)), row,
                   pl.BlockSpec((None, None, d, tm), lambda b, i: (b, i, 0, 0)),
                   pl.BlockSpec((None, 1, d), lambda b, i: (b * nb + i, 0, 0))),
        compiler_params=_cparams("parallel", "parallel"),
        name="moba_qkv",
    )(x, qshift, qscale, qng, kshift, kscale, kng, wq, wk, wv, qg, kg)


def _moba_attn_kernel(qt_ref, k_ref, vt_ref, bias_ref, o_ref, qs_ref, sa_ref, sb_ref, p_ref):
    blk = MOBA_BLOCK
    nb = bias_ref.shape[1]
    qb = pl.program_id(2)
    heads = range(qs_ref.shape[0])
    group = lambda hd: slice(hd // HEADS_PER_LANE_GROUP * LANES, (hd // HEADS_PER_LANE_GROUP + 1) * LANES)
    neg_inf = F32(-jnp.inf)
    chan_head = lax.broadcasted_iota(jnp.int32, (LANES, blk), 0) // HEAD_DIM
    key_i = lax.broadcasted_iota(jnp.int32, (blk, blk), 0)
    qry_i = lax.broadcasted_iota(jnp.int32, (blk, blk), 1)
    causal = key_i <= qry_i
    q_scale = F32(HEAD_DIM ** -0.5 * LOG2_E)
    for hd in heads:
        keep = chan_head == hd % HEADS_PER_LANE_GROUP
        qs_ref[hd] = (jnp.where(keep, qt_ref[group(hd), :], 0.0) * q_scale).astype(BF16)

    def put_scores(n, dst_ref):
        rows = pl.ds(pl.multiple_of(n * blk, blk), blk)
        for hd in heads:
            dst_ref[hd] = _dot(k_ref[rows, group(hd)], qs_ref[hd])

    def bias_rows(n):
        return [bias_ref[hd, pl.ds(n, 1), :] for hd in heads]

    ones_rows = jnp.ones((SUM_ROWS, blk), BF16)

    def weighted_values(n, p, rows):
        vtn = vt_ref[n]
        lhs = [jnp.concatenate([vtn[hd * HEAD_DIM:(hd + 1) * HEAD_DIM, :], ones_rows], axis=0)
               for hd in heads]
        return [jnp.where(rows[hd] == 0.0, _dot(lhs[hd], p[hd]), 0.0) for hd in heads]

    def softmax_step(src_ref, m, rows):
        s = [src_ref[hd] for hd in heads]
        m_new = [jnp.maximum(m[hd], jnp.max(s[hd], axis=0, keepdims=True) + rows[hd]) for hd in heads]
        alpha = [jnp.exp2(m[hd] - m_new[hd]) for hd in heads]
        p = [jnp.exp2(s[hd] - m_new[hd]).astype(BF16) for hd in heads]
        return m_new, alpha, p

    past_of = lambda t: jnp.clip(t - 1, 0, nb - 1)
    values_of = lambda t: jnp.where(t == 0, qb, past_of(t))
    rows_of = lambda t: [jnp.where(t == 0, 0.0, r) for r in bias_rows(past_of(t))]

    def tile_pair(j, carry):
        m, acc, alpha_prev = carry
        a, b = 2 * j, 2 * j + 1
        prev = jnp.maximum(a - 1, 0)
        rows_a, rows_b = rows_of(a), rows_of(b)
        put_scores(past_of(b), sb_ref)
        pv_prev = weighted_values(values_of(prev), [p_ref[hd] for hd in heads], rows_of(prev))
        m, alpha_a, p_a = softmax_step(sa_ref, m, rows_a)
        put_scores(past_of(b + 1), sa_ref)
        pv_a = weighted_values(values_of(a), p_a, rows_a)
        m, alpha_b, p_b = softmax_step(sb_ref, m, rows_b)
        for hd in heads:
            p_ref[hd] = p_b[hd]
        acc = [alpha_a[hd] * (alpha_prev[hd] * acc[hd] + pv_prev[hd]) + pv_a[hd] for hd in heads]
        return m, acc, alpha_b

    own = pl.ds(pl.multiple_of(qb * blk, blk), blk)
    for hd in heads:
        sa_ref[hd] = jnp.where(causal, _dot(k_ref[own, group(hd)], qs_ref[hd]), neg_inf)
    p_ref[...] = jnp.zeros_like(p_ref)
    init = ([jnp.full((1, blk), neg_inf, F32) for _ in heads],
            [jnp.zeros((HEAD_DIM + SUM_ROWS, blk), F32) for _ in heads],
            [jnp.ones((1, blk), F32) for _ in heads])
    n_pairs = (qb + 2) // 2
    m, acc, alpha_prev = lax.fori_loop(0, n_pairs, tile_pair, init)
    last = 2 * n_pairs - 1
    pv_last = weighted_values(values_of(last), [p_ref[hd] for hd in heads], rows_of(last))
    acc = [alpha_prev[hd] * acc[hd] + pv_last[hd] for hd in heads]
    ot = jnp.concatenate([acc[hd][:HEAD_DIM] / acc[hd][HEAD_DIM:HEAD_DIM + 1] for hd in heads], axis=0)
    o_ref[...] = ot.T


def _moba_attn(qt, k, vt, bias):
    bsz, d, t = qt.shape
    nb = t // MOBA_BLOCK
    width = ATTN_GROUPS_PER_STEP * LANES
    hpg = ATTN_GROUPS_PER_STEP * HEADS_PER_LANE_GROUP
    return pl.pallas_call(
        _moba_attn_kernel,
        out_shape=jax.ShapeDtypeStruct((bsz, t, d), F32),
        grid=(bsz, d // width, nb),
        in_specs=[pl.BlockSpec((None, width, MOBA_BLOCK), lambda b, hp, i: (b, hp, i)),
                  pl.BlockSpec((None, t, width), lambda b, hp, i: (b, 0, hp)),
                  pl.BlockSpec((None, nb, width, MOBA_BLOCK), lambda b, hp, i: (b, 0, hp, 0)),
                  pl.BlockSpec((None, None, hpg, nb, MOBA_BLOCK), lambda b, hp, i: (b, i, hp, 0, 0))],
        out_specs=pl.BlockSpec((None, MOBA_BLOCK, width), lambda b, hp, i: (b, i, hp)),
        scratch_shapes=[pltpu.VMEM((hpg, LANES, MOBA_BLOCK), BF16),
                        pltpu.VMEM((hpg, MOBA_BLOCK, MOBA_BLOCK), F32),
                        pltpu.VMEM((hpg, MOBA_BLOCK, MOBA_BLOCK), F32),
                        pltpu.VMEM((hpg, MOBA_BLOCK, MOBA_BLOCK), BF16)],
        compiler_params=_cparams("parallel", "parallel", "arbitrary"),
        name="moba_attn",
    )(qt, k, vt, bias)


def _moba_select_kernel(qt_ref, km_ref, bias_ref):
    nb = km_ref.shape[0]
    blk = qt_ref.shape[1]
    qb = pl.program_id(1)
    neg_inf = F32(-jnp.inf)
    rows = HEADS_PER_LANE_GROUP * nb
    row_head = lax.broadcasted_iota(jnp.int32, (rows, LANES), 0) // nb
    chan_head = lax.broadcasted_iota(jnp.int32, (rows, LANES), 1) // HEAD_DIM
    gates = []
    for g in range(qt_ref.shape[0] // LANES):
        km = km_ref[:, g * LANES:(g + 1) * LANES]
        km2 = jnp.where(row_head == chan_head, jnp.concatenate([km] * HEADS_PER_LANE_GROUP, axis=0), 0.0)
        gates.append(jnp.dot(km2, qt_ref[g * LANES:(g + 1) * LANES, :], precision=HIGHEST,
                             preferred_element_type=F32))
    gate = jnp.concatenate(gates, axis=0).reshape(N_HEADS, nb, blk)
    blk_iota = lax.broadcasted_iota(jnp.int32, gate.shape, 1)
    valid = blk_iota < qb
    gate = jnp.where(valid, gate, neg_inf)
    sel = jnp.zeros(gate.shape, jnp.bool_)
    for _ in range(MOBA_TOPK):
        mx = jnp.max(gate, axis=1, keepdims=True)
        idx = jnp.min(jnp.where(gate == mx, blk_iota, nb), axis=1, keepdims=True)
        hit = blk_iota == idx
        sel = jnp.logical_or(sel, hit)
        gate = jnp.where(hit, neg_inf, gate)
    bias_ref[...] = jnp.where(jnp.logical_and(sel, valid), 0.0, neg_inf)


def _moba_select(qt, km):
    bsz, d, t = qt.shape
    nb = km.shape[1]
    return pl.pallas_call(
        _moba_select_kernel,
        out_shape=jax.ShapeDtypeStruct((bsz, nb, N_HEADS, nb, MOBA_BLOCK), F32),
        grid=(bsz, nb),
        in_specs=[pl.BlockSpec((None, d, MOBA_BLOCK), lambda b, i: (b, 0, i)),
                  pl.BlockSpec((None, nb, d), lambda b, i: (b, 0, 0))],
        out_specs=pl.BlockSpec((None, None, N_HEADS, nb, MOBA_BLOCK), lambda b, i: (b, i, 0, 0, 0)),
        compiler_params=_cparams("parallel", "parallel"),
        name="moba_select",
    )(qt, km)


def _pad_rank(w_down, w_up):
    rank = w_down.shape[1]
    pad = -rank % LANES
    return (jnp.pad(w_down, ((0, 0), (0, pad))).astype(BF16),
            jnp.pad(w_up, ((0, pad), (0, 0))).astype(BF16))


def kernel(x, c, norm_g, w_ada, b_ada, rw_mu, rw_w_rkv, rw_w0, rw_w1, rw_w2, rw_a0, rw_a1, rw_a2, rw_g1, rw_g2, rw_k_k, rw_k_a, rw_r_k, rw_lnx_g, rw_lnx_b, rw_w_o, ffn_w_gate, ffn_w_up, ffn_w_down, kv_norm_g, kv_w_ada, kv_b_ada, kv_w_k, kv_w_v, k_norm_g, mb_w_q, mb_q_norm_g, mb_w_o):
    depth = norm_g.shape[0]
    d = x.shape[-1]
    assert depth == 2 and rw_mu.shape[0] == 1 and mb_w_q.shape[0] == 1 and d == D_MODEL

    mods = _ada(c, w_ada.reshape(2 * depth, d, 3 * d), b_ada.reshape(2 * depth, 1, 3 * d))
    kv_mods = _ada(c, kv_w_ada[None], kv_b_ada[None, None])[0]
    part = lambda m, j: m[:, None, j * d:(j + 1) * d]
    vec = lambda a: a.reshape(1, d)

    m = mods[0]
    w1, w2 = _pad_rank(rw_w1[0], rw_w2[0])
    a1, a2 = _pad_rank(rw_a1[0], rw_a2[0])
    g1, g2 = _pad_rank(rw_g1[0], rw_g2[0])
    r, k, v, ld, kkn, kb, g = _rwkv_proj(
        x, part(m, 0), part(m, 1), vec(norm_g[0, 0]), rw_mu[0], rw_w_rkv[0].astype(BF16),
        vec(rw_w0[0]), w1, w2, vec(rw_a0[0]), a1, a2, g1, g2, vec(rw_k_k[0]), vec(rw_k_a[0]))
    y = _rwkv_scan(r, k, v, ld, kkn, kb)
    f = mods[1]
    x = _mixer_ffn((y, r, k, v, g), (vec(rw_lnx_g[0]), vec(rw_lnx_b[0]), vec(rw_r_k[0])),
                   x, part(m, 2), rw_w_o[0].astype(BF16),
                   part(f, 0), part(f, 1), part(f, 2), vec(norm_g[0, 1]),
                   ffn_w_gate[0].astype(BF16), ffn_w_up[0].astype(BF16), ffn_w_down[0].astype(BF16))

    m = mods[2]
    q, kx, vx, km = _moba_qkv(
        x, part(m, 0), part(m, 1), vec(norm_g[1, 0]), part(kv_mods, 0), part(kv_mods, 1),
        vec(kv_norm_g), mb_w_q[0].astype(BF16), kv_w_k.astype(BF16), kv_w_v.astype(BF16),
        vec(jnp.tile(mb_q_norm_g[0], N_HEADS)), vec(jnp.tile(k_norm_g, N_HEADS)))
    o = _moba_attn(q, kx, vx, _moba_select(q, km.reshape(x.shape[0], -1, d)))
    f = mods[3]
    return _mixer_ffn((o,), (), x, part(m, 2), mb_w_o[0].astype(BF16),
                      part(f, 0), part(f, 1), part(f, 2), vec(norm_g[1, 1]),
                      ffn_w_gate[1].astype(BF16), ffn_w_up[1].astype(BF16), ffn_w_down[1].astype(BF16))
```

```python
import jax
import jax.numpy as jnp
from jax import lax
from jax.experimental import pallas as pl
from jax.experimental.pallas import tpu as pltpu

F32 = jnp.float32
BF16 = jnp.bfloat16
HIGHEST = lax.Precision.HIGHEST

D_MODEL = 1024
HEAD_DIM = 64
N_HEADS = D_MODEL // HEAD_DIM
MOBA_BLOCK = 256
MOBA_TOPK = 3
NORM_EPS = 1e-6
GN_EPS = 64e-5
L2_EPS = 1e-12
LOG2_E = 1.4426950408889634

LANES = 128
MXU_WIDTH = 256
HEADS_PER_LANE_GROUP = LANES // HEAD_DIM
VMEM_LIMIT_BYTES = 56 * 1024 * 1024
SCAN_CHUNK = 64
ROW_TILE = 256
FFN_ROW_TILE = 256
SCAN_TIME_BLOCK = 512
SCAN_GROUPS_PER_STEP = 4
SCAN_UNROLL = 2
ATTN_GROUPS_PER_STEP = 1
SUM_ROWS = 16
N_FFN_REFS = 7


def _cparams(*sem):
    return pltpu.CompilerParams(dimension_semantics=sem, vmem_limit_bytes=VMEM_LIMIT_BYTES)


def _dot(a, b):
    return jnp.dot(a, b, preferred_element_type=F32)


def _dot_nt(a, b, precision=None):
    return lax.dot_general(a, b, (((1,), (1,)), ((), ())), precision=precision,
                           preferred_element_type=F32)


def _dot_tn(a, b):
    return lax.dot_general(a, b, (((0,), (0,)), ((), ())), preferred_element_type=F32)


def _head_ones(width):
    r = lax.broadcasted_iota(jnp.int32, (width, width), 0) // HEAD_DIM
    c = lax.broadcasted_iota(jnp.int32, (width, width), 1) // HEAD_DIM
    return (r == c).astype(BF16)


def _split_bf16(x, terms):
    parts = []
    for _ in range(terms - 1):
        hi = x.astype(BF16)
        parts.append(hi)
        x = x - hi.astype(F32)
    return parts + [x.astype(BF16)]


def _head_sum(x):
    xb = x.astype(BF16)
    ones = _head_ones(MXU_WIDTH)
    parts = [_dot(xb[:, j * MXU_WIDTH:(j + 1) * MXU_WIDTH], ones)
             for j in range(x.shape[1] // MXU_WIDTH)]
    return jnp.concatenate(parts, axis=1)


def _ada_ln(x, g, shift, scale):
    inv = lax.rsqrt(jnp.mean(x * x, axis=-1, keepdims=True) + NORM_EPS)
    return (x * inv * g) * (1.0 + scale) + shift


def _ada_kernel(c_ref, w_ref, b_ref, o_ref):
    c = c_ref[...]
    s = c * jax.nn.sigmoid(c)
    o_ref[...] = jnp.dot(s, w_ref[...], precision=HIGHEST, preferred_element_type=F32) + b_ref[...]


def _ada(c, w, b):
    n, d, md = w.shape
    bsz = c.shape[0]
    return pl.pallas_call(
        _ada_kernel,
        out_shape=jax.ShapeDtypeStruct((n, bsz, md), F32),
        grid=(n, md // d),
        in_specs=[pl.BlockSpec((bsz, d), lambda i, j: (0, 0)),
                  pl.BlockSpec((None, d, d), lambda i, j: (i, 0, j)),
                  pl.BlockSpec((None, 1, d), lambda i, j: (i, 0, j))],
        out_specs=pl.BlockSpec((None, bsz, d), lambda i, j: (i, 0, j)),
        compiler_params=_cparams("parallel", "parallel"),
        name="ada",
    )(c, w, b)


def _rwkv_proj_kernel(x_ref, shift_ref, scale_ref, ng_ref, mu_ref, wrkv_ref, w0_ref, w1_ref,
                      w2_ref, a0_ref, a1_ref, a2_ref, g1_ref, g2_ref, kk_ref, ka_ref,
                      r_out, k_out, v_out, ld_out, kkn_out, kb_out, g_out, carry_ref):
    tm = x_ref.shape[0]
    h = _ada_ln(x_ref[...], ng_ref[...], shift_ref[...], scale_ref[...])
    first = pl.program_id(1) == 0
    prev_last = jnp.where(first, 0.0, carry_ref[7:8, :])
    row = lax.broadcasted_iota(jnp.int32, h.shape, 0)
    h_prev = jnp.where(row == 0, prev_last, pltpu.roll(h, 1, axis=0))
    carry_ref[...] = h[tm - 8:tm, :]
    d = h_prev - h
    mix = lambda i: (h + d * mu_ref[i:i + 1, :]).astype(BF16)

    r = _dot(mix(0), wrkv_ref[0])
    k = _dot(mix(1), wrkv_ref[1])
    v = _dot(mix(2), wrkv_ref[2])
    zw = w0_ref[...] + _dot(jnp.tanh(_dot(mix(3), w1_ref[...])).astype(BF16), w2_ref[...])
    a = jax.nn.sigmoid(a0_ref[...] + _dot(_dot(mix(4), a1_ref[...]).astype(BF16), a2_ref[...]))
    g = _dot(jax.nn.sigmoid(_dot(mix(5), g1_ref[...])).astype(BF16), g2_ref[...])

    ld_out[...] = -jnp.exp(F32(-0.5)) * jax.nn.sigmoid(zw)
    kk = k * kk_ref[...]
    nrm = jnp.maximum(jnp.sqrt(_head_sum(kk * kk)), L2_EPS)
    kk = kk / nrm
    r_out[...] = r.astype(r_out.dtype)
    k_out[...] = (k * (1.0 + (a - 1.0) * ka_ref[...])).astype(k_out.dtype)
    v_out[...] = v.astype(v_out.dtype)
    kkn_out[...] = kk.astype(kkn_out.dtype)
    kb_out[...] = (kk * a).astype(kb_out.dtype)
    g_out[...] = g.astype(g_out.dtype)


def _rwkv_proj(x, shift, scale, ng, mu, wrkv, w0, w1, w2, a0, a1, a2, g1, g2, k_k, k_a):
    bsz, t, d = x.shape
    tm = ROW_TILE
    row = pl.BlockSpec((None, tm, d), lambda b, i: (b, i, 0))
    mod = pl.BlockSpec((None, 1, d), lambda b, i: (b, 0, 0))
    vec = pl.BlockSpec((1, d), lambda b, i: (0, 0))
    full = lambda a: pl.BlockSpec(a.shape, lambda b, i: (0,) * a.ndim)
    out = lambda dt: jax.ShapeDtypeStruct((bsz, t, d), dt)
    return pl.pallas_call(
        _rwkv_proj_kernel,
        out_shape=(out(BF16), out(BF16), out(BF16), out(F32), out(BF16), out(BF16), out(BF16)),
        grid=(bsz, t // tm),
        in_specs=[row, mod, mod, vec, full(mu), full(wrkv), vec, full(w1), full(w2), vec,
                  full(a1), full(a2), full(g1), full(g2), vec, vec],
        out_specs=(row,) * 7,
        scratch_shapes=[pltpu.VMEM((8, d), F32)],
        compiler_params=_cparams("arbitrary", "arbitrary"),
        name="rwkv_proj",
    )(x, shift, scale, ng, mu, wrkv, w0, w1, w2, a0, a1, a2, g1, g2, k_k, k_a)


def _rwkv_scan_kernel(r_ref, k_ref, v_ref, ld_ref, kk_ref, kb_ref, y_ref,
                      s_ref, ry_ref, y1_ref, rr_ref, q_ref, g_ref):
    c = SCAN_CHUNK
    n_chunks = r_ref.shape[0] // c
    n_groups = r_ref.shape[1] // LANES
    two_c = HEADS_PER_LANE_GROUP * c

    @pl.when(pl.program_id(2) == 0)
    def _():
        s_ref[...] = jnp.zeros_like(s_ref)

    ri = lax.broadcasted_iota(jnp.int32, (two_c, two_c), 0)
    ci = lax.broadcasted_iota(jnp.int32, (two_c, two_c), 1)
    same_head = (ri // c) == (ci // c)
    strict = jnp.logical_and(same_head, ri > ci)
    incl = jnp.logical_and(same_head, ri >= ci)
    both_tri = jnp.concatenate([strict, incl], axis=0)
    eye = (ri == ci).astype(F32)
    ti = lax.broadcasted_iota(jnp.int32, (c, c), 0)
    tj = lax.broadcasted_iota(jnp.int32, (c, c), 1)
    tri3 = jnp.concatenate([(ti >= tj).astype(BF16)] * 3, axis=1)
    cumsum = lambda z: _dot(tri3, jnp.concatenate(_split_bf16(z, 3), axis=0))
    head0 = lax.broadcasted_iota(jnp.int32, (c, LANES), 1) < HEAD_DIM

    def stack(z):
        return jnp.concatenate([jnp.where(head0, z, 0.0), jnp.where(head0, 0.0, z)], axis=0)

    def prepare(chains, hooks=(lambda: None, lambda: None)):
        each = lambda f, *cols: [f(*args) for args in zip(*cols)]
        rows = [pl.ds(pl.multiple_of(ic * c, c), c) for ic, _ in chains]
        lanes = [slice(gi * LANES, (gi + 1) * LANES) for _, gi in chains]
        load = lambda ref: each(lambda rw, ln: ref[rw, ln].astype(F32), rows, lanes)
        ld = load(ld_ref)
        cum = each(cumsum, ld)
        tot = each(lambda z: z[c - 1:c, :], cum)
        kb, kx = load(kb_ref), load(k_ref)
        g_inv = each(lambda z: jnp.exp(-z), cum)
        g_rest = each(lambda t_, z: jnp.exp(t_ - z), tot, cum)
        at = each(lambda kn, z, l_: stack(-kn * jnp.exp(z - l_)), load(kk_ref), cum, ld)
        rt = each(lambda r_, z: stack(r_ * jnp.exp(z)), load(r_ref), cum)
        bh = each(lambda x_, g_: stack(x_ * g_), kb, g_inv)
        kh = each(lambda x_, g_: stack(x_ * g_), kx, g_inv)
        bc = each(lambda x_, g_: stack(x_ * g_), kb, g_rest)
        kc = each(lambda x_, g_: stack(x_ * g_), kx, g_rest)
        vs = each(stack, load(v_ref))
        bf = lambda zs: [z.astype(BF16) for z in zs]
        at_b, bc_b, vs_b = bf(at), bf(bc), bf(vs)
        hooks[0]()
        qa = each(lambda a_, r_: jnp.concatenate([a_, r_.astype(BF16)], axis=0), at_b, rt)
        bk = each(lambda b_, k_: jnp.concatenate([b_, k_], axis=0), bf(bh), bf(kh))
        mbk = each(_dot_nt, qa, bk)
        a_rb = bf(each(lambda z: jnp.where(incl, z[two_c:, :LANES], 0.0), mbk))
        akk = bf(each(lambda z: jnp.where(both_tri, z[:, LANES:], 0.0), mbk))
        akk_v = each(_dot, akk, vs_b)
        av = bf(each(lambda z: z[:two_c], akk_v))
        pw = each(lambda z: jnp.where(strict, z[:two_c, :LANES], 0.0), mbk)
        t_inv = each(lambda z: eye + z, pw)
        pw_b = bf(pw)
        pw_b = bf(each(_dot, pw_b, pw_b))
        for _ in range(4):
            both = each(lambda p_, t_: _dot(jnp.concatenate([p_, t_.astype(BF16)], axis=0), p_),
                        pw_b, t_inv)
            t_inv = each(lambda t_, z: t_ + z[two_c:], t_inv, both)
            pw_b = bf(each(lambda z: z[:two_c], both))
        t_inv = each(lambda t_, p_: t_ + _dot(t_.astype(BF16), p_), t_inv, pw_b)
        hooks[1]()
        wu = bf(each(lambda t_, a_, v_: _dot(t_, jnp.concatenate([a_, v_], axis=1)), bf(t_inv), at_b, av))
        arb_wu = each(_dot, a_rb, wu)
        wu_bc = each(_dot_tn, wu, bc_b)
        ry = each(lambda r_, z: r_ + z[:, :LANES], rt, arb_wu)
        y1 = each(lambda z, w_: z[:, LANES:] + w_[two_c:], arb_wu, akk_v)
        rr = each(lambda z: z[:two_c], wu_bc)
        qq = each(lambda z, v_, k_: z[two_c:] + _dot_tn(v_, k_), wu_bc, vs_b, bf(kc))
        for j, (ic, gi) in enumerate(chains):
            ry_ref[gi, ic] = ry[j].astype(BF16)
            y1_ref[gi, ic] = y1[j][:c] + y1[j][c:]
            rr_ref[gi, ic] = rr[j].astype(BF16)
            q_ref[gi, ic] = qq[j]
            g_ref[gi, ic] = jnp.broadcast_to(jnp.exp(tot[j]), (8, LANES))

    def advance(ic, states):
        rows = pl.ds(pl.multiple_of(ic * c, c), c)
        groups = range(n_groups)
        s_b = [states[gi].astype(BF16) for gi in groups]
        sr = [_dot(s_b[gi], rr_ref[gi, ic]) for gi in groups]
        ys = [_dot_nt(ry_ref[gi, ic], s_b[gi]) for gi in groups]
        for gi in groups:
            y = ys[gi][:c] + ys[gi][c:] + y1_ref[gi, ic]
            y_ref[rows, gi * LANES:(gi + 1) * LANES] = y.astype(y_ref.dtype)
        return tuple(states[gi] * g_ref[gi, ic, 0:1, :] + sr[gi] + q_ref[gi, ic] for gi in groups)

    chains_of = lambda ig: [(ig * SCAN_UNROLL + u, gi) for u in range(SCAN_UNROLL)
                            for gi in range(n_groups)]
    n_iter = n_chunks // SCAN_UNROLL

    def pipelined(ig, states):
        holder = [states]

        def advance_hook(u):
            def run():
                holder[0] = advance((ig - 1) * SCAN_UNROLL + u, holder[0])
            return run

        prepare(chains_of(ig), hooks=(advance_hook(0), advance_hook(1)))
        return holder[0]

    prepare(chains_of(0))
    states = lax.fori_loop(1, n_iter, pipelined, tuple(s_ref[gi] for gi in range(n_groups)))
    for u in range(SCAN_UNROLL):
        states = advance((n_iter - 1) * SCAN_UNROLL + u, states)
    for gi in range(n_groups):
        s_ref[gi] = states[gi]


def _rwkv_scan(r, k, v, ld, kkn, kb):
    bsz, t, d = r.shape
    tb = SCAN_TIME_BLOCK
    n_groups = SCAN_GROUPS_PER_STEP
    n_chunks = tb // SCAN_CHUNK
    two_c = HEADS_PER_LANE_GROUP * SCAN_CHUNK
    blk = pl.BlockSpec((None, tb, n_groups * LANES), lambda b, hp, i: (b, i, hp))
    mats = lambda dt: pltpu.VMEM((n_groups, n_chunks, two_c, LANES), dt)
    return pl.pallas_call(
        _rwkv_scan_kernel,
        out_shape=jax.ShapeDtypeStruct((bsz, t, d), BF16),
        grid=(bsz, d // (n_groups * LANES), t // tb),
        in_specs=[blk] * 6,
        out_specs=blk,
        scratch_shapes=[pltpu.VMEM((n_groups, two_c, LANES), F32), mats(BF16),
                        pltpu.VMEM((n_groups, n_chunks, SCAN_CHUNK, LANES), F32), mats(BF16),
                        mats(F32),
                        pltpu.VMEM((n_groups, n_chunks, 8, LANES), F32)],
        compiler_params=_cparams("parallel", "parallel", "arbitrary"),
        name="rwkv_scan",
    )(r, k, v, ld, kkn, kb)


def _rwkv_gated(y_ref, r_ref, k_ref, v_ref, g_ref, lg_ref, lb_ref, rk_ref):
    f32 = lambda ref: ref[...].astype(F32)
    y = f32(y_ref)
    mean = _head_sum(y) * (1.0 / HEAD_DIM)
    dy = y - mean
    var = _head_sum(dy * dy) * (1.0 / HEAD_DIM)
    yn = dy * lax.rsqrt(var + GN_EPS) * lg_ref[...] + lb_ref[...]
    bonus = _head_sum(f32(r_ref) * f32(k_ref) * rk_ref[...]) * f32(v_ref)
    return ((yn + bonus) * f32(g_ref)).astype(BF16)


def _swiglu_residual(x, shift_ref, scale_ref, gate_ref, ng_ref, wg_ref, wu_ref, wd_ref):
    h = _ada_ln(x, ng_ref[...], shift_ref[...], scale_ref[...]).astype(BF16)
    gt = _dot(h, wg_ref[...])
    up = _dot(h, wu_ref[...])
    act = (gt * jax.nn.sigmoid(gt) * up).astype(BF16)
    return x + gate_ref[...] * _dot(act, wd_ref[...])


def _mixer_ffn_kernel(*refs):
    *mixer_refs, x_ref, agate_ref, wo_ref = refs[:-N_FFN_REFS - 1]
    ffn_refs, o_ref = refs[-N_FFN_REFS - 1:-1], refs[-1]
    a = mixer_refs[0][...].astype(BF16) if len(mixer_refs) == 1 else _rwkv_gated(*mixer_refs)
    x = x_ref[...] + agate_ref[...] * _dot(a, wo_ref[...])
    o_ref[...] = _swiglu_residual(x, *ffn_refs)


def _mixer_ffn(rows, vecs, x, agate, wo, shift, scale, gate, ng, wg, wu, wd):
    bsz, t, d = x.shape
    f = wg.shape[1]
    tm = FFN_ROW_TILE
    row = pl.BlockSpec((None, tm, d), lambda b, i: (b, i, 0))
    mod = pl.BlockSpec((None, 1, d), lambda b, i: (b, 0, 0))
    vec = pl.BlockSpec((1, d), lambda b, i: (0, 0))
    const = lambda shape: pl.BlockSpec(shape, lambda b, i: (0, 0), pipeline_mode=pl.Buffered(1))
    specs = ([row] * len(rows) + [vec] * len(vecs) + [row, mod, const((d, d))]
             + [mod, mod, mod, vec, const((d, f)), const((d, f)), const((f, d))])
    assert len(specs) == len(rows) + len(vecs) + 3 + N_FFN_REFS
    return pl.pallas_call(
        _mixer_ffn_kernel,
        out_shape=jax.ShapeDtypeStruct((bsz, t, d), F32),
        grid=(bsz, t // tm),
        in_specs=specs,
        out_specs=row,
        compiler_params=_cparams("parallel", "parallel"),
        name="mixer_ffn",
    )(*rows, *vecs, x, agate, wo, shift, scale, gate, ng, wg, wu, wd)


def _head_rms(z, g):
    ms = _head_sum(z * z) * (1.0 / HEAD_DIM)
    return z * lax.rsqrt(ms + NORM_EPS) * g


def _moba_qkv_kernel(x_ref, qshift_ref, qscale_ref, qng_ref, kshift_ref, kscale_ref, kng_ref,
                     wq_ref, wk_ref, wv_ref, qg_ref, kg_ref, qt_out, k_out, vt_out, km_out):
    x = x_ref[...]
    xn = x * lax.rsqrt(jnp.mean(x * x, axis=-1, keepdims=True) + NORM_EPS)
    hq = ((xn * qng_ref[...]) * (1.0 + qscale_ref[...]) + qshift_ref[...]).astype(BF16)
    hk = ((xn * kng_ref[...]) * (1.0 + kscale_ref[...]) + kshift_ref[...]).astype(BF16)
    qt_out[...] = _head_rms(_dot(hq, wq_ref[...]), qg_ref[...]).T
    k = _head_rms(_dot(hk, wk_ref[...]), kg_ref[...])
    k_out[...] = k.astype(BF16)
    vt_out[...] = _dot(hk, wv_ref[...]).T.astype(BF16)
    km_out[...] = jnp.mean(k, axis=0, keepdims=True)


def _moba_qkv(x, qshift, qscale, qng, kshift, kscale, kng, wq, wk, wv, qg, kg):
    bsz, t, d = x.shape
    tm = MOBA_BLOCK
    nb = t // tm
    row = pl.BlockSpec((None, tm, d), lambda b, i: (b, i, 0))
    mod = pl.BlockSpec((None, 1, d), lambda b, i: (b, 0, 0))
    vec = pl.BlockSpec((1, d), lambda b, i: (0, 0))
    mat = pl.BlockSpec((d, d), lambda b, i: (0, 0))
    return pl.pallas_call(
        _moba_qkv_kernel,
        out_shape=(jax.ShapeDtypeStruct((bsz, d, t), F32),
                   jax.ShapeDtypeStruct((bsz, t, d), BF16),
                   jax.ShapeDtypeStruct((bsz, nb, d, tm), BF16),
                   jax.ShapeDtypeStruct((bsz * nb, 1, d), F32)),
        grid=(bsz, nb),
        in_specs=[row, mod, mod, vec, mod, mod, vec, mat, mat, mat, vec, vec],
        out_specs=(pl.BlockSpec((None, d, tm), lambda b, i: (b, 0, i)), row,
                   pl.BlockSpec((None, None, d, tm), lambda b, i: (b, i, 0, 0)),
                   pl.BlockSpec((None, 1, d), lambda b, i: (b * nb + i, 0, 0))),
        compiler_params=_cparams("parallel", "parallel"),
        name="moba_qkv",
    )(x, qshift, qscale, qng, kshift, kscale, kng, wq, wk, wv, qg, kg)


def _moba_attn_kernel(qt_ref, k_ref, vt_ref, bias_ref, o_ref, qs_ref, sa_ref, sb_ref, p_ref):
    blk = MOBA_BLOCK
    nb = bias_ref.shape[1]
    qb = pl.program_id(2)
    heads = range(qs_ref.shape[0])
    group = lambda hd: slice(hd // HEADS_PER_LANE_GROUP * LANES, (hd // HEADS_PER_LANE_GROUP + 1) * LANES)
    neg_inf = F32(-jnp.inf)
    chan_head = lax.broadcasted_iota(jnp.int32, (LANES, blk), 0) // HEAD_DIM
    key_i = lax.broadcasted_iota(jnp.int32, (blk, blk), 0)
    qry_i = lax.broadcasted_iota(jnp.int32, (blk, blk), 1)
    causal = key_i <= qry_i
    q_scale = F32(HEAD_DIM ** -0.5 * LOG2_E)
    for hd in heads:
        keep = chan_head == hd % HEADS_PER_LANE_GROUP
        qs_ref[hd] = (jnp.where(keep, qt_ref[group(hd), :], 0.0) * q_scale).astype(BF16)

    def put_scores(n, dst_ref):
        rows = pl.ds(pl.multiple_of(n * blk, blk), blk)
        for hd in heads:
            dst_ref[hd] = _dot(k_ref[rows, group(hd)], qs_ref[hd])

    def bias_rows(n):
        return [bias_ref[hd, pl.ds(n, 1), :] for hd in heads]

    ones_rows = jnp.ones((SUM_ROWS, blk), BF16)

    def weighted_values(n, p, rows):
        vtn = vt_ref[n]
        lhs = [jnp.concatenate([vtn[hd * HEAD_DIM:(hd + 1) * HEAD_DIM, :], ones_rows], axis=0)
               for hd in heads]
        return [jnp.where(rows[hd] == 0.0, _dot(lhs[hd], p[hd]), 0.0) for hd in heads]

    def softmax_step(src_ref, m, rows):
        s = [src_ref[hd] for hd in heads]
        m_new = [jnp.maximum(m[hd], jnp.max(s[hd], axis=0, keepdims=True) + rows[hd]) for hd in heads]
        alpha = [jnp.exp2(m[hd] - m_new[hd]) for hd in heads]
        p = [jnp.exp2(s[hd] - m_new[hd]).astype(BF16) for hd in heads]
        return m_new, alpha, p

    past_of = lambda t: jnp.clip(t - 1, 0, nb - 1)
    values_of = lambda t: jnp.where(t == 0, qb, past_of(t))
    rows_of = lambda t: [jnp.where(t == 0, 0.0, r) for r in bias_rows(past_of(t))]

    def tile_pair(j, carry):
        m, acc, alpha_prev = carry
        a, b = 2 * j, 2 * j + 1
        prev = jnp.maximum(a - 1, 0)
        rows_a, rows_b = rows_of(a), rows_of(b)
        put_scores(past_of(b), sb_ref)
        pv_prev = weighted_values(values_of(prev), [p_ref[hd] for hd in heads], rows_of(prev))
        m, alpha_a, p_a = softmax_step(sa_ref, m, rows_a)
        put_scores(past_of(b + 1), sa_ref)
        pv_a = weighted_values(values_of(a), p_a, rows_a)
        m, alpha_b, p_b = softmax_step(sb_ref, m, rows_b)
        for hd in heads:
            p_ref[hd] = p_b[hd]
        acc = [alpha_a[hd] * (alpha_prev[hd] * acc[hd] + pv_prev[hd]) + pv_a[hd] for hd in heads]
        return m, acc, alpha_b

    own = pl.ds(pl.multiple_of(qb * blk, blk), blk)
    for hd in heads:
        sa_ref[hd] = jnp.where(causal, _dot(k_ref[own, group(hd)], qs_ref[hd]), neg_inf)
    p_ref[...] = jnp.zeros_like(p_ref)
    init = ([jnp.full((1, blk), neg_inf, F32) for _ in heads],
            [jnp.zeros((HEAD_DIM + SUM_ROWS, blk), F32) for _ in heads],
            [jnp.ones((1, blk), F32) for _ in heads])
    n_pairs = (qb + 2) // 2
    m, acc, alpha_prev = lax.fori_loop(0, n_pairs, tile_pair, init)
    last = 2 * n_pairs - 1
    pv_last = weighted_values(values_of(last), [p_ref[hd] for hd in heads], rows_of(last))
    acc = [alpha_prev[hd] * acc[hd] + pv_last[hd] for hd in heads]
    ot = jnp.concatenate([acc[hd][:HEAD_DIM] / acc[hd][HEAD_DIM:HEAD_DIM + 1] for hd in heads], axis=0)
    o_ref[...] = ot.T


def _moba_attn(qt, k, vt, bias):
    bsz, d, t = qt.shape
    nb = t // MOBA_BLOCK
    width = ATTN_GROUPS_PER_STEP * LANES
    hpg = ATTN_GROUPS_PER_STEP * HEADS_PER_LANE_GROUP
    return pl.pallas_call(
        _moba_attn_kernel,
        out_shape=jax.ShapeDtypeStruct((bsz, t, d), F32),
        grid=(bsz, d // width, nb),
        in_specs=[pl.BlockSpec((None, width, MOBA_BLOCK), lambda b, hp, i: (b, hp, i)),
                  pl.BlockSpec((None, t, width), lambda b, hp, i: (b, 0, hp)),
                  pl.BlockSpec((None, nb, width, MOBA_BLOCK), lambda b, hp, i: (b, 0, hp, 0)),
                  pl.BlockSpec((None, None, hpg, nb, MOBA_BLOCK), lambda b, hp, i: (b, i, hp, 0, 0))],
        out_specs=pl.BlockSpec((None, MOBA_BLOCK, width), lambda b, hp, i: (b, i, hp)),
        scratch_shapes=[pltpu.VMEM((hpg, LANES, MOBA_BLOCK), BF16),
                        pltpu.VMEM((hpg, MOBA_BLOCK, MOBA_BLOCK), F32),
                        pltpu.VMEM((hpg, MOBA_BLOCK, MOBA_BLOCK), F32),
                        pltpu.VMEM((hpg, MOBA_BLOCK, MOBA_BLOCK), BF16)],
        compiler_params=_cparams("parallel", "parallel", "arbitrary"),
        name="moba_attn",
    )(qt, k, vt, bias)


def _moba_select_kernel(qt_ref, km_ref, bias_ref):
    nb = km_ref.shape[0]
    blk = qt_ref.shape[1]
    qb = pl.program_id(1)
    neg_inf = F32(-jnp.inf)
    rows = HEADS_PER_LANE_GROUP * nb
    row_head = lax.broadcasted_iota(jnp.int32, (rows, LANES), 0) // nb
    chan_head = lax.broadcasted_iota(jnp.int32, (rows, LANES), 1) // HEAD_DIM
    gates = []
    for g in range(qt_ref.shape[0] // LANES):
        km = km_ref[:, g * LANES:(g + 1) * LANES]
        km2 = jnp.where(row_head == chan_head, jnp.concatenate([km] * HEADS_PER_LANE_GROUP, axis=0), 0.0)
        gates.append(jnp.dot(km2, qt_ref[g * LANES:(g + 1) * LANES, :], precision=HIGHEST,
                             preferred_element_type=F32))
    gate = jnp.concatenate(gates, axis=0).reshape(N_HEADS, nb, blk)
    blk_iota = lax.broadcasted_iota(jnp.int32, gate.shape, 1)
    valid = blk_iota < qb
    gate = jnp.where(valid, gate, neg_inf)
    sel = jnp.zeros(gate.shape, jnp.bool_)
    for _ in range(MOBA_TOPK):
        mx = jnp.max(gate, axis=1, keepdims=True)
        idx = jnp.min(jnp.where(gate == mx, blk_iota, nb), axis=1, keepdims=True)
        hit = blk_iota == idx
        sel = jnp.logical_or(sel, hit)
        gate = jnp.where(hit, neg_inf, gate)
    bias_ref[...] = jnp.where(jnp.logical_and(sel, valid), 0.0, neg_inf)


def _moba_select(qt, km):
    bsz, d, t = qt.shape
    nb = km.shape[1]
    return pl.pallas_call(
        _moba_select_kernel,
        out_shape=jax.ShapeDtypeStruct((bsz, nb, N_HEADS, nb, MOBA_BLOCK), F32),
        grid=(bsz, nb),
        in_specs=[pl.BlockSpec((None, d, MOBA_BLOCK), lambda b, i: (b, 0, i)),
                  pl.BlockSpec((None, nb, d), lambda b, i: (b, 0, 0))],
        out_specs=pl.BlockSpec((None, None, N_HEADS, nb, MOBA_BLOCK), lambda b, i: (b, i, 0, 0, 0)),
        compiler_params=_cparams("parallel", "parallel"),
        name="moba_select",
    )(qt, km)


def _pad_rank(w_down, w_up):
    rank = w_down.shape[1]
    pad = -rank % LANES
    return (jnp.pad(w_down, ((0, 0), (0, pad))).astype(BF16),
            jnp.pad(w_up, ((0, pad), (0, 0))).astype(BF16))


def kernel(x, c, norm_g, w_ada, b_ada, rw_mu, rw_w_rkv, rw_w0, rw_w1, rw_w2, rw_a0, rw_a1, rw_a2, rw_g1, rw_g2, rw_k_k, rw_k_a, rw_r_k, rw_lnx_g, rw_lnx_b, rw_w_o, ffn_w_gate, ffn_w_up, ffn_w_down, kv_norm_g, kv_w_ada, kv_b_ada, kv_w_k, kv_w_v, k_norm_g, mb_w_q, mb_q_norm_g, mb_w_o):
    depth = norm_g.shape[0]
    d = x.shape[-1]
    assert depth == 2 and rw_mu.shape[0] == 1 and mb_w_q.shape[0] == 1 and d == D_MODEL

    mods = _ada(c, w_ada.reshape(2 * depth, d, 3 * d), b_ada.reshape(2 * depth, 1, 3 * d))
    kv_mods = _ada(c, kv_w_ada[None], kv_b_ada[None, None])[0]
    part = lambda m, j: m[:, None, j * d:(j + 1) * d]
    vec = lambda a: a.reshape(1, d)

    m = mods[0]
    w1, w2 = _pad_rank(rw_w1[0], rw_w2[0])
    a1, a2 = _pad_rank(rw_a1[0], rw_a2[0])
    g1, g2 = _pad_rank(rw_g1[0], rw_g2[0])
    r, k, v, ld, kkn, kb, g = _rwkv_proj(
        x, part(m, 0), part(m, 1), vec(norm_g[0, 0]), rw_mu[0], rw_w_rkv[0].astype(BF16),
        vec(rw_w0[0]), w1, w2, vec(rw_a0[0]), a1, a2, g1, g2, vec(rw_k_k[0]), vec(rw_k_a[0]))
    y = _rwkv_scan(r, k, v, ld, kkn, kb)
    f = mods[1]
    x = _mixer_ffn((y, r, k, v, g), (vec(rw_lnx_g[0]), vec(rw_lnx_b[0]), vec(rw_r_k[0])),
                   x, part(m, 2), rw_w_o[0].astype(BF16),
                   part(f, 0), part(f, 1), part(f, 2), vec(norm_g[0, 1]),
                   ffn_w_gate[0].astype(BF16), ffn_w_up[0].astype(BF16), ffn_w_down[0].astype(BF16))

    m = mods[2]
    q, kx, vx, km = _moba_qkv(
        x, part(m, 0), part(m, 1), vec(norm_g[1, 0]), part(kv_mods, 0), part(kv_mods, 1),
        vec(kv_norm_g), mb_w_q[0].astype(BF16), kv_w_k.astype(BF16), kv_w_v.astype(BF16),
        vec(jnp.tile(mb_q_norm_g[0], N_HEADS)), vec(jnp.tile(k_norm_g, N_HEADS)))
    o = _moba_attn(q, kx, vx, _moba_select(q, km.reshape(x.shape[0], -1, d)))
    f = mods[3]
    return _mixer_ffn((o,), (), x, part(m, 2), mb_w_o[0].astype(BF16),
                      part(f, 0), part(f, 1), part(f, 2), vec(norm_g[1, 1]),
                      ffn_w_gate[1].astype(BF16), ffn_w_up[1].astype(BF16), ffn_w_down[1].astype(BF16))
```
